```python
import math
import jax
import jax.numpy as jnp
from jax import lax
import numpy as np

D_MODEL = 1024
BATCH = 8
SEQ = 4096
DEPTH = 1
DEC_BATCH = 128
DEC_SEQ = 1
PAST_LEN = 8192
PAGE_SIZE = 128

GLA_HEADS = 4
GLA_HEAD_K = D_MODEL // 16
GLA_HEAD_V = D_MODEL // 8
GLA_KEY = GLA_HEADS * GLA_HEAD_K
GLA_VAL = GLA_HEADS * GLA_HEAD_V
GLA_LOWRANK = 16
GLA_TAU = 16.0
GLA_CHUNK = 64
SB_HEADS = D_MODEL // 128
SB_HEAD_DIM = 64
SB_WIDTH = SB_HEADS * SB_HEAD_DIM
SB_QBLOCK = 128
SB_LOGIT_OFFSET = -6.5
N_EXPERTS = 32
TOP_K = 4
EXPERT_FF = D_MODEL
SWIGLU_LIMIT = 7.0
SWIGLU_ALPHA = 1.702
MOE_BLOCK = 128
RMS_EPS = 1e-6
N_ADA = 6
IN_SPLITS = (GLA_KEY, GLA_KEY, GLA_VAL, GLA_VAL, GLA_LOWRANK, SB_WIDTH, SB_WIDTH, SB_WIDTH, D_MODEL, D_MODEL)
IN_WIDTH = 2 * GLA_KEY + 2 * GLA_VAL + GLA_LOWRANK + 3 * SB_WIDTH + 2 * D_MODEL

kernel_name = 'gla_stickbreak_moe_hybrid_step'


def rmsnorm(x, g):
    xf = x.astype(jnp.float32)
    y = xf * lax.rsqrt(jnp.mean(xf * xf, axis=-1, keepdims=True) + RMS_EPS)
    return (y * g.astype(jnp.float32)).astype(x.dtype)


def gla_chunked(q, k, v, log_a, s0):
    B, T, H, _ = q.shape
    dv = v.shape[-1]
    C = math.gcd(GLA_CHUNK, T)
    nc = T // C

    def to_chunks(a):
        return a.astype(jnp.float32).reshape(B, nc, C, H, a.shape[-1]).transpose(1, 0, 3, 2, 4)

    qc, kc, vc, ac = to_chunks(q), to_chunks(k), to_chunks(v), to_chunks(log_a)
    causal = jnp.tril(jnp.ones((C, C), dtype=bool))

    def step(s, inp):
        qi, ki, vi, ai = inp
        b = jnp.cumsum(ai, axis=2)
        b_last = b[:, :, -1:, :]
        q_t = qi * jnp.exp(b)
        k_t = ki * jnp.exp(-b)
        scores = jnp.where(causal, jnp.einsum('bhqd,bhkd->bhqk', q_t, k_t), 0.0)
        o = jnp.einsum('bhqd,bhdv->bhqv', q_t, s) + jnp.einsum('bhqk,bhkv->bhqv', scores, vi)
        k_upd = ki * jnp.exp(b_last - b)
        s_new = jnp.exp(b_last)[:, :, 0, :, None] * s + jnp.einsum('bhkd,bhkv->bhdv', k_upd, vi)
        return s_new, o

    s_fin, o = lax.scan(step, s0.astype(jnp.float32), (qc, kc, vc, ac))
    o = o.transpose(1, 0, 3, 2, 4).reshape(B, T, H, dv)
    return o, s_fin


def sb_block(q_blk, pos_blk, segs, logit_bias):
    scale = SB_HEAD_DIM ** -0.5
    z = jnp.concatenate(
        [jnp.einsum('bqhd,bkhd->bhqk', q_blk, k).astype(jnp.float32) * scale for k, _, _ in segs], axis=-1)
    z = z + logit_bias.astype(jnp.float32)[None, :, None, None]
    k_pos = jnp.concatenate([kp for _, _, kp in segs])
    mask = k_pos[None, :] < pos_blk[:, None]
    log_1m_beta = jnp.where(mask, jax.nn.log_sigmoid(-z), 0.0)
    later = lax.cumsum(log_1m_beta, axis=3, reverse=True) - log_1m_beta
    a = jnp.where(mask, jnp.exp(jax.nn.log_sigmoid(z) + later), 0.0)
    out = None
    off = 0
    for k, v, _ in segs:
        n = k.shape[1]
        part = jnp.einsum('bhqk,bkhd->bqhd', a[..., off:off + n].astype(v.dtype), v)
        out = part if out is None else out + part
        off += n
    return out


def sb_attend(q, q_pos, segs, logit_bias):
    B, T, H, d = q.shape
    blk = min(SB_QBLOCK, T)
    nb = -(-T // blk)
    pad = nb * blk - T
    q = jnp.pad(q, ((0, 0), (0, pad), (0, 0), (0, 0)))
    q_pos = jnp.pad(q_pos, (0, pad), constant_values=-1)
    qb = q.reshape(B, nb, blk, H, d).transpose(1, 0, 2, 3, 4)
    pb = q_pos.reshape(nb, blk)
    out = lax.map(lambda a: sb_block(a[0], a[1], segs, logit_bias), (qb, pb))
    dv = out.shape[-1]
    return out.transpose(1, 0, 2, 3, 4).reshape(B, nb * blk, H, dv)[:, :T]


def moe(h, w_router, b_router, w_gu, b_gu, w_down, b_down):
    N, D = h.shape
    logits = (h @ w_router).astype(jnp.float32) + b_router.astype(jnp.float32)
    top_v, top_i = lax.top_k(logits, TOP_K)
    probs = jax.nn.softmax(top_v, axis=-1)
    NK = N * TOP_K
    e_flat = top_i.reshape(NK)
    w_flat = probs.reshape(NK)
    tok_flat = jnp.arange(NK, dtype=jnp.int32) // TOP_K
    order = jnp.argsort(e_flat, stable=True)
    e_sorted = e_flat[order]
    counts = jnp.bincount(e_flat, length=N_EXPERTS)
    padded = (counts + MOE_BLOCK - 1) // MOE_BLOCK * MOE_BLOCK
    start = jnp.cumsum(counts) - counts
    pend = jnp.cumsum(padded)
    pstart = pend - padded
    dest = pstart[e_sorted] + (jnp.arange(NK) - start[e_sorted])
    n_blocks = -(-NK // MOE_BLOCK) + N_EXPERTS
    R = n_blocks * MOE_BLOCK
    row_tok = jnp.zeros((R,), jnp.int32).at[dest].set(tok_flat[order])
    row_w = jnp.zeros((R,), jnp.float32).at[dest].set(w_flat[order])
    block_e = jnp.searchsorted(pend, jnp.arange(n_blocks) * MOE_BLOCK, side='right')
    block_e = jnp.minimum(block_e, N_EXPERTS - 1)
    xb = h[row_tok].reshape(n_blocks, MOE_BLOCK, D)

    def expert_block(args):
        xi, e = args
        gu = xi @ w_gu[e] + b_gu[e]
        g, u = jnp.split(gu, 2, axis=-1)
        g = jnp.minimum(g, SWIGLU_LIMIT)
        u = jnp.clip(u, -SWIGLU_LIMIT, SWIGLU_LIMIT)
        act = (u + 1.0) * (g * jax.nn.sigmoid(SWIGLU_ALPHA * g))
        return act @ w_down[e] + b_down[e]

    yb = lax.map(expert_block, (xb, block_e)).reshape(R, D)
    out = jnp.zeros((N, D), jnp.float32).at[row_tok].add(yb.astype(jnp.float32) * row_w[:, None])
    return out.astype(h.dtype)


def decoder_layer(x, c, s0, past_k, past_v, past_len, p):
    B, T, _ = x.shape
    ada = (jax.nn.silu(c) @ p['w_ada'] + p['b_ada']).reshape(B, N_ADA, 1, D_MODEL)
    shift1, scale1, gate1, shift2, scale2, gate2 = [ada[:, i] for i in range(N_ADA)]

    u = rmsnorm(x, p['g_pre_mix']) * (1.0 + scale1) + shift1
    proj = u @ p['w_in']
    split_pts = tuple(int(s) for s in np.cumsum(IN_SPLITS)[:-1])
    q_a, k_a, v_a, r_a, lr_a, q_b, k_b, v_b, gl_a, gl_b = jnp.split(proj, split_pts, axis=-1)

    log_a = jax.nn.log_sigmoid((lr_a @ p['w_alpha'] + p['b_alpha']).astype(jnp.float32)) / GLA_TAU
    o_a, s_new = gla_chunked(
        q_a.reshape(B, T, GLA_HEADS, GLA_HEAD_K) * (GLA_HEAD_K ** -0.5),
        k_a.reshape(B, T, GLA_HEADS, GLA_HEAD_K),
        v_a.reshape(B, T, GLA_HEADS, GLA_HEAD_V),
        log_a.reshape(B, T, GLA_HEADS, GLA_HEAD_K),
        s0)
    o_a = rmsnorm(o_a.astype(x.dtype), p['g_gla_head']).reshape(B, T, GLA_VAL) * jax.nn.silu(r_a)
    br_a = o_a @ p['w_gla_o']

    qb = q_b.reshape(B, T, SB_HEADS, SB_HEAD_DIM)
    kb = k_b.reshape(B, T, SB_HEADS, SB_HEAD_DIM)
    vb = v_b.reshape(B, T, SB_HEADS, SB_HEAD_DIM)
    pos_q = past_len + jnp.arange(T, dtype=jnp.int32)
    segs = ((kb, vb, pos_q),)
    if past_k is not None:
        segs = ((past_k, past_v, jnp.arange(past_len, dtype=jnp.int32)),) + segs
    o_b = sb_attend(qb, pos_q, segs, p['b_sb_logit']).reshape(B, T, SB_WIDTH).astype(x.dtype)
    br_b = o_b @ p['w_sb_o']

    mixed = (jax.nn.sigmoid(gl_a) * br_a + jax.nn.sigmoid(gl_b) * br_b) @ p['w_out']
    x = x + gate1 * rmsnorm(mixed, p['g_post_mix'])

    h = rmsnorm(x, p['g_pre_ffn']) * (1.0 + scale2) + shift2
    f = moe(h.reshape(B * T, D_MODEL), p['w_router'], p['b_router'], p['w_gate_up'], p['b_gate_up'],
            p['w_down'], p['b_down']).reshape(B, T, D_MODEL)
    x = x + gate2 * rmsnorm(f, p['g_post_ffn'])
    return x, kb, vb, s_new


def setup_inputs(seed: int = 0) -> dict:
    key = jax.random.key(seed)
    ks = jax.random.split(key, 32)
    n_pages = PAST_LEN // PAGE_SIZE
    n_used = DEC_BATCH * n_pages
    n_pool = n_used + (n_used + 3) // 4
    nrm = jax.random.normal
    L = DEPTH

    def gain(k, n):
        return 1.0 + 0.05 * nrm(k, (L, n), jnp.float32)

    page_table = jax.random.permutation(ks[0], n_pool)[:n_used].reshape(DEC_BATCH, n_pages).astype(jnp.int32)
    return {
        'x_prompt': nrm(ks[1], (BATCH, SEQ, D_MODEL), jnp.float32),
        'x_sample': nrm(ks[2], (DEC_BATCH, DEC_SEQ, D_MODEL), jnp.float32),
        'cache_sb_k': nrm(ks[3], (L, n_pool, PAGE_SIZE, SB_HEADS, SB_HEAD_DIM), jnp.float32),
        'cache_sb_v': nrm(ks[4], (L, n_pool, PAGE_SIZE, SB_HEADS, SB_HEAD_DIM), jnp.float32),
        'state_gla': nrm(ks[5], (L, DEC_BATCH, GLA_HEADS, GLA_HEAD_K, GLA_HEAD_V), jnp.float32),
        'page_table': page_table,
        'c_prompt': nrm(ks[6], (BATCH, D_MODEL), jnp.float32),
        'c_sample': nrm(ks[7], (DEC_BATCH, D_MODEL), jnp.float32),
        'w_ada': nrm(ks[8], (L, D_MODEL, N_ADA * D_MODEL), jnp.float32) * (0.5 * D_MODEL ** -0.5),
        'b_ada': 0.02 * nrm(ks[9], (L, N_ADA * D_MODEL), jnp.float32),
        'g_pre_mix': gain(ks[10], D_MODEL),
        'w_in': nrm(ks[11], (L, D_MODEL, IN_WIDTH), jnp.float32) * D_MODEL ** -0.5,
        'w_alpha': nrm(ks[12], (L, GLA_LOWRANK, GLA_KEY), jnp.float32) * GLA_LOWRANK ** -0.5,
        'b_alpha': 0.1 * nrm(ks[13], (L, GLA_KEY), jnp.float32),
        'g_gla_head': gain(ks[14], GLA_HEAD_V),
        'w_gla_o': nrm(ks[15], (L, GLA_VAL, D_MODEL), jnp.float32) * GLA_VAL ** -0.5,
        'b_sb_logit': SB_LOGIT_OFFSET + 0.1 * nrm(ks[27], (L, SB_HEADS), jnp.float32),
        'w_sb_o': nrm(ks[16], (L, SB_WIDTH, D_MODEL), jnp.float32) * SB_WIDTH ** -0.5,
        'w_out': nrm(ks[17], (L, D_MODEL, D_MODEL), jnp.float32) * D_MODEL ** -0.5,
        'g_post_mix': gain(ks[18], D_MODEL),
        'g_pre_ffn': gain(ks[19], D_MODEL),
        'w_router': nrm(ks[20], (L, D_MODEL, N_EXPERTS), jnp.float32) * D_MODEL ** -0.5,
        'b_router': 0.01 * nrm(ks[21], (L, N_EXPERTS), jnp.float32),
        'w_gate_up': nrm(ks[22], (L, N_EXPERTS, D_MODEL, 2 * EXPERT_FF), jnp.float32) * D_MODEL ** -0.5,
        'b_gate_up': 0.02 * nrm(ks[23], (L, N_EXPERTS, 2 * EXPERT_FF), jnp.float32),
        'w_down': nrm(ks[24], (L, N_EXPERTS, EXPERT_FF, D_MODEL), jnp.float32) * EXPERT_FF ** -0.5,
        'b_down': 0.02 * nrm(ks[25], (L, N_EXPERTS, D_MODEL), jnp.float32),
        'g_post_ffn': gain(ks[26], D_MODEL),
    }


def reference(x_prompt, x_sample, cache_sb_k, cache_sb_v, state_gla, page_table, c_prompt, c_sample,
              w_ada, b_ada, g_pre_mix, w_in, w_alpha, b_alpha, g_gla_head, w_gla_o, b_sb_logit, w_sb_o, w_out,
              g_post_mix, g_pre_ffn, w_router, b_router, w_gate_up, b_gate_up, w_down, b_down, g_post_ffn):
    n_seq_pages = page_table.shape[1]
    page = cache_sb_k.shape[2]
    past_len = n_seq_pages * page
    dec_b = x_sample.shape[0]
    yp, ys = x_prompt, x_sample
    kp_l, vp_l, sp_l, ks_l, vs_l, ss_l = [], [], [], [], [], []
    for l in range(DEPTH):
        p = {
            'w_ada': w_ada[l], 'b_ada': b_ada[l], 'g_pre_mix': g_pre_mix[l], 'w_in': w_in[l],
            'w_alpha': w_alpha[l], 'b_alpha': b_alpha[l], 'g_gla_head': g_gla_head[l],
            'w_gla_o': w_gla_o[l], 'b_sb_logit': b_sb_logit[l], 'w_sb_o': w_sb_o[l], 'w_out': w_out[l],
            'g_post_mix': g_post_mix[l], 'g_pre_ffn': g_pre_ffn[l], 'w_router': w_router[l],
            'b_router': b_router[l], 'w_gate_up': w_gate_up[l], 'b_gate_up': b_gate_up[l],
            'w_down': w_down[l], 'b_down': b_down[l], 'g_post_ffn': g_post_ffn[l],
        }
        s0_p = jnp.zeros((yp.shape[0], GLA_HEADS, GLA_HEAD_K, GLA_HEAD_V), jnp.float32)
        yp, kp, vp, sp = decoder_layer(yp, c_prompt, s0_p, None, None, 0, p)
        past_k = cache_sb_k[l][page_table].reshape(dec_b, past_len, SB_HEADS, SB_HEAD_DIM)
        past_v = cache_sb_v[l][page_table].reshape(dec_b, past_len, SB_HEADS, SB_HEAD_DIM)
        ys, kn, vn, sn = decoder_layer(ys, c_sample, state_gla[l], past_k, past_v, past_len, p)
        kp_l.append(kp)
        vp_l.append(vp)
        sp_l.append(sp)
        ks_l.append(kn)
        vs_l.append(vn)
        ss_l.append(sn)
    return (yp, ys, jnp.stack(kp_l), jnp.stack(vp_l), jnp.stack(sp_l), jnp.stack(ks_l), jnp.stack(vs_l), jnp.stack(ss_l))
```

```python
import functools

import jax
import jax.numpy as jnp
from jax import lax
from jax.experimental import pallas as pl
from jax.experimental.pallas import tpu as pltpu

F32 = jnp.float32
BF16 = jnp.bfloat16
I32 = jnp.int32

D_MODEL = 1024
GLA_HEADS = 4
GLA_HEAD_K = 64
GLA_HEAD_V = 128
GLA_KEY = GLA_HEADS * GLA_HEAD_K
GLA_VAL = GLA_HEADS * GLA_HEAD_V
GLA_LOWRANK = 16
GLA_TAU = 16.0
GLA_CHUNK = 64
SB_HEADS = 8
SB_HEAD_DIM = 64
SB_WIDTH = SB_HEADS * SB_HEAD_DIM
N_EXPERTS = 32
TOP_K = 4
EXPERT_FF = D_MODEL
SWIGLU_LIMIT = 7.0
SWIGLU_ALPHA = 1.702
RMS_EPS = 1e-6
N_ADA = 6

LANES = 128
VMEM_LIMIT = 56 * 1024 * 1024
MOE_ROWS = 256
DECODE_PAGES = 8


def _params(n_axes, vmem=VMEM_LIMIT):
    return pltpu.CompilerParams(dimension_semantics=("arbitrary",) * n_axes, vmem_limit_bytes=vmem)


def _sigmoid(x):
    return 1.0 / (1.0 + jnp.exp(-x))


def _softplus(z):
    return jnp.maximum(z, 0.0) + jnp.log(1.0 + jnp.exp(-jnp.abs(z)))


def _rms(x, g):
    ms = jnp.mean(x * x, axis=-1, keepdims=True)
    return x * lax.rsqrt(ms + RMS_EPS) * g


def _dot(a, b):
    return jnp.dot(a, b, preferred_element_type=F32)


def _dot_nt(a, b):
    return lax.dot_general(a, b, (((1,), (1,)), ((), ())), preferred_element_type=F32)


def _dot_tn(a, b):
    return lax.dot_general(a, b, (((0,), (0,)), ((), ())), preferred_element_type=F32)


def _ada_kernel(c_ref, w_ref, b_ref, o_ref):
    c = c_ref[...]
    s = c * _sigmoid(c)
    o_ref[...] = jnp.dot(s, w_ref[...], preferred_element_type=F32,
                         precision=lax.Precision.HIGHEST) + b_ref[...]


def _ada_call(c_all, w_ada, b_ada):
    rows = c_all.shape[0]
    n_out = w_ada.shape[1]
    tn = D_MODEL
    return pl.pallas_call(
        _ada_kernel,
        grid=(n_out // tn,),
        in_specs=[pl.BlockSpec((rows, D_MODEL), lambda j: (0, 0)),
                  pl.BlockSpec((D_MODEL, tn), lambda j: (0, j)),
                  pl.BlockSpec((1, tn), lambda j: (0, j))],
        out_specs=pl.BlockSpec((rows, tn), lambda j: (0, j)),
        out_shape=jax.ShapeDtypeStruct((rows, n_out), F32),
        compiler_params=_params(1),
        name="ada",
    )(c_all, w_ada, b_ada)


def _inproj_kernel(x_ref, sh_ref, sc_ref, g_ref, wg_ref, wlr_ref, wal_ref, bal_ref, wsb_ref, wgt_ref,
                   qa_ref, ka_ref, la_ref, va_ref, ra_ref, qb_ref, kb_ref, vb_ref, kf_ref, vf_ref,
                   sga_ref, sgb_ref):
    x = x_ref[...]
    u = _rms(x, g_ref[...]) * (1.0 + sc_ref[0]) + sh_ref[0]
    ub = u.astype(BF16)

    pg = _dot(ub, wg_ref[...])
    qa_ref[...] = pg[:, :GLA_KEY] * (GLA_HEAD_K ** -0.5)
    ka_ref[...] = pg[:, GLA_KEY:2 * GLA_KEY]
    va_ref[...] = pg[:, 2 * GLA_KEY:2 * GLA_KEY + GLA_VAL].astype(BF16)
    r = pg[:, 2 * GLA_KEY + GLA_VAL:]
    ra_ref[...] = (r * _sigmoid(r)).astype(BF16)

    lr = _dot(ub, wlr_ref[...])
    al = _dot(lr.astype(BF16), wal_ref[...]) + bal_ref[...]
    la_ref[...] = -_softplus(-al) * (1.0 / GLA_TAU)

    ps = _dot(ub, wsb_ref[...])
    qb_ref[...] = (ps[:, :SB_WIDTH] * (SB_HEAD_DIM ** -0.5)).astype(BF16)
    k = ps[:, SB_WIDTH:2 * SB_WIDTH]
    v = ps[:, 2 * SB_WIDTH:]
    kf_ref[...] = k
    vf_ref[...] = v
    kb_ref[...] = k.astype(BF16)
    vb_ref[...] = v.astype(BF16)

    pt = _dot(ub, wgt_ref[...])
    sga_ref[...] = _sigmoid(pt[:, :D_MODEL]).astype(BF16)
    sgb_ref[...] = _sigmoid(pt[:, D_MODEL:]).astype(BF16)


def _mod_spec(mod, tiles_per_mod):
    return pl.BlockSpec((1,) + mod.shape[1:], lambda i: (i // tiles_per_mod, 0, 0))


def _const_spec(a):
    nd = a.ndim
    return pl.BlockSpec(a.shape, lambda *_: (0,) * nd)


def _inproj_call(x2d, shift, scale, gain, w, tm, tiles_per_mod):
    n = x2d.shape[0]
    widths = [(GLA_KEY, F32), (GLA_KEY, F32), (GLA_KEY, F32), (GLA_VAL, BF16), (GLA_VAL, BF16),
              (SB_WIDTH, BF16), (SB_WIDTH, BF16), (SB_WIDTH, BF16), (SB_WIDTH, F32), (SB_WIDTH, F32),
              (D_MODEL, BF16), (D_MODEL, BF16)]
    row = lambda i: (i, 0)
    consts = [gain, w["gla"], w["lr"], w["alpha"], w["b_alpha"], w["sb"], w["gates"]]
    return pl.pallas_call(
        _inproj_kernel,
        grid=(n // tm,),
        in_specs=[pl.BlockSpec((tm, D_MODEL), row), _mod_spec(shift, tiles_per_mod),
                  _mod_spec(scale, tiles_per_mod)] + [_const_spec(a) for a in consts],
        out_specs=[pl.BlockSpec((tm, wd), row) for wd, _ in widths],
        out_shape=[jax.ShapeDtypeStruct((n, wd), dt) for wd, dt in widths],
        compiler_params=_params(1),
        name="inproj",
    )(x2d, shift, scale, *consts)


def _gla_kernel(q_ref, k_ref, la_ref, v_ref, r_ref, s0_ref, g_ref, o_ref, sn_ref, st_sc, *, n_chunks):
    j = pl.program_id(1)
    C = GLA_CHUNK

    @pl.when(j == 0)
    def _():
        for p in range(2):
            s_pair = jnp.concatenate([s0_ref[0, 2 * p], s0_ref[0, 2 * p + 1]], axis=0)
            st_sc[p] = s_pair.T

    row = lax.broadcasted_iota(I32, (C, C), 0)
    col = lax.broadcasted_iota(I32, (C, C), 1)
    causal = col <= row
    l_incl = jnp.where(causal, 1.0, 0.0).astype(BF16)
    lane = lax.broadcasted_iota(I32, (C, LANES), 1)
    lane_sq = lax.broadcasted_iota(I32, (LANES, LANES), 1)

    def chunk(c, carry):
        sl = pl.ds(pl.multiple_of(c * C, C), C)
        a = la_ref[sl, :]
        a_hi = a.astype(BF16)
        a_lo = (a - a_hi.astype(F32)).astype(BF16)
        b = _dot(l_incl, a_hi) + _dot(l_incl, a_lo)
        b_last = b[C - 1:C, :]
        q_t = q_ref[sl, :] * jnp.exp(b)
        k = k_ref[sl, :]
        k_t = (k * jnp.exp(-b)).astype(BF16)
        k_u = (k * jnp.exp(b_last - b)).astype(BF16)
        dec = jnp.exp(b_last)
        for p in range(2):
            ps = slice(LANES * p, LANES * (p + 1))
            qp = q_t[:, ps]
            kp = k_t[:, ps]
            kup = k_u[:, ps]
            st = st_sc[p]
            stb = st.astype(BF16)
            upd = []
            for hh in range(2):
                h = 2 * p + hh
                hs = slice(GLA_HEAD_V * h, GLA_HEAD_V * (h + 1))
                in_head = (lane >= GLA_HEAD_K * hh) & (lane < GLA_HEAD_K * (hh + 1))
                qm = jnp.where(in_head, qp, 0.0).astype(BF16)
                s = jnp.where(causal, _dot_nt(qm, kp), 0.0)
                vh = v_ref[sl, hs]
                o = _dot_nt(qm, stb) + _dot(s.astype(BF16), vh)
                on = _rms(o, g_ref[...])
                o_ref[sl, hs] = (on * r_ref[sl, hs].astype(F32)).astype(BF16)
                upd.append(_dot_tn(vh, kup))
            st_sc[p] = st * dec[:, ps] + jnp.where(lane_sq < GLA_HEAD_K, upd[0], upd[1])
        return carry

    lax.fori_loop(0, n_chunks, chunk, 0)

    @pl.when(j == pl.num_programs(1) - 1)
    def _():
        for p in range(2):
            s_pair = st_sc[p].T
            sn_ref[0, 2 * p] = s_pair[:GLA_HEAD_K]
            sn_ref[0, 2 * p + 1] = s_pair[GLA_HEAD_K:]


def _gla_call(qa, ka, la, va, ra, s0, g_head, batch, seq, tt):
    n = qa.shape[0]
    nt = seq // tt
    rowmap = lambda b, j: (b * nt + j, 0)
    smap = lambda b, j: (b, 0, 0, 0)
    sblock = (1, GLA_HEADS, GLA_HEAD_K, GLA_HEAD_V)
    return pl.pallas_call(
        functools.partial(_gla_kernel, n_chunks=tt // GLA_CHUNK),
        grid=(batch, nt),
        in_specs=[pl.BlockSpec((tt, GLA_KEY), rowmap), pl.BlockSpec((tt, GLA_KEY), rowmap),
                  pl.BlockSpec((tt, GLA_KEY), rowmap), pl.BlockSpec((tt, GLA_VAL), rowmap),
                  pl.BlockSpec((tt, GLA_VAL), rowmap), pl.BlockSpec(sblock, smap),
                  pl.BlockSpec((1, GLA_HEAD_V), lambda b, j: (0, 0))],
        out_specs=[pl.BlockSpec((tt, GLA_VAL), rowmap), pl.BlockSpec(sblock, smap)],
        out_shape=[jax.ShapeDtypeStruct((n, GLA_VAL), BF16),
                   jax.ShapeDtypeStruct((batch,) + sblock[1:], F32)],
        scratch_shapes=[pltpu.VMEM((2, LANES, LANES), F32)],
        compiler_params=_params(2),
        name="gla",
    )(qa, ka, la, va, ra, s0, g_head)


def _sb_prompt_kernel(bias_ref, q_ref, k_ref, v_ref, o_ref, *, tq):
    pair = pl.program_id(1)
    qi = pl.program_id(2)
    q2 = q_ref[...]
    lane = lax.broadcasted_iota(I32, (tq, LANES), 1)
    row = lax.broadcasted_iota(I32, (tq, tq), 0)
    col = lax.broadcasted_iota(I32, (tq, tq), 1)
    strict = row < col
    later = jnp.where(col > row, 1.0, 0.0).astype(BF16)
    accs = []
    for p in range(2):
        bias = bias_ref[2 * pair + p]
        in_head = (lane >= SB_HEAD_DIM * p) & (lane < SB_HEAD_DIM * (p + 1))
        qm = jnp.where(in_head, q2, jnp.zeros_like(q2))

        def block(j, masked, acc, run):
            ks = pl.ds(pl.multiple_of(j * tq, tq), tq)
            kblk = k_ref[ks, :]
            vblk = v_ref[ks, :]
            z = _dot_nt(kblk, qm) + bias
            sp = _softplus(z)
            if masked:
                sp = jnp.where(strict, sp, 0.0)
            cum = _dot(later, sp.astype(BF16)) + run
            a = jnp.exp(z - sp - cum)
            if masked:
                a = jnp.where(strict, a, 0.0)
            acc = acc + _dot_tn(vblk, a.astype(BF16))
            run = cum[0:1, :] + sp[0:1, :]
            return acc, run

        acc0 = jnp.zeros((LANES, tq), F32)
        run0 = jnp.zeros((1, tq), F32)
        acc, run = block(qi, True, acc0, run0)
        acc, run = lax.fori_loop(1, qi + 1, lambda jj, c: block(qi - jj, False, *c), (acc, run))
        accs.append(acc)
    rows = lax.broadcasted_iota(I32, (LANES, tq), 0)
    o_t = jnp.where(rows < SB_HEAD_DIM, accs[0], accs[1])
    o_ref[...] = o_t.T.astype(BF16)


def _sb_prompt_call(qb, kb, vb, bias, batch, seq, tq):
    n = qb.shape[0]
    nq = seq // tq
    return pl.pallas_call(
        functools.partial(_sb_prompt_kernel, tq=tq),
        grid=(batch, SB_HEADS // 2, nq),
        in_specs=[pl.BlockSpec(memory_space=pltpu.SMEM),
                  pl.BlockSpec((tq, LANES), lambda b, p, i: (b * nq + i, p)),
                  pl.BlockSpec((seq, LANES), lambda b, p, i: (b, p)),
                  pl.BlockSpec((seq, LANES), lambda b, p, i: (b, p))],
        out_specs=pl.BlockSpec((tq, LANES), lambda b, p, i: (b * nq + i, p)),
        out_shape=jax.ShapeDtypeStruct((n, SB_WIDTH), BF16),
        compiler_params=_params(3),
        name="sb_prompt",
    )(bias, qb, kb, vb)


def _sb_decode_kernel(pt_ref, qc_ref, bias_ref, tri_ref, *refs, n_pages):
    k_refs = refs[:n_pages]
    v_refs = refs[n_pages:2 * n_pages]
    o_ref, acc_sc, run_sc, z_sc = refs[2 * n_pages:]
    g = pl.program_id(1)
    heads, d, page = k_refs[0].shape

    @pl.when(g == 0)
    def _():
        acc_sc[...] = jnp.zeros_like(acc_sc)
        run_sc[...] = jnp.zeros_like(run_sc)

    bias = bias_ref[...]
    run = run_sc[:, 0:1]
    for i in range(n_pages):
        for h in range(heads):
            prod = k_refs[i][h] * qc_ref[0, h]
            z_sc[h:h + 1, :] = jnp.sum(prod, axis=0, keepdims=True)
        z = z_sc[...] + bias
        sp = _softplus(z)
        cum = _dot(sp.astype(BF16), tri_ref[...]) + run
        a = jnp.exp(z - sp - cum)
        run = run + jnp.sum(sp, axis=1, keepdims=True)
        for h in range(heads):
            acc_sc[h] += v_refs[i][h] * a[h:h + 1, :]
    run_sc[...] = jnp.broadcast_to(run, run_sc.shape)

    @pl.when(g == pl.num_programs(1) - 1)
    def _():
        ones = jnp.ones((8, page), F32)
        for h in range(heads):
            tot = lax.dot_general(ones, acc_sc[h], (((1,), (1,)), ((), ())), preferred_element_type=F32,
                                  precision=lax.Precision.HIGHEST)
            o_ref[0, h:h + 1, :] = tot[0:1, :]


def _sb_decode_call(qc, bias_col, tri, cache_k_t, cache_v_t, page_table, layer):
    nb, heads, d, page = qc.shape
    n_seq_pages = page_table.shape[1]
    gp = DECODE_PAGES
    while n_seq_pages % gp:
        gp //= 2
    ng = n_seq_pages // gp

    def page_spec(i):
        def imap(b, g, pt):
            return (layer, pt[b, n_seq_pages - 1 - (g * gp + i)], 0, 0, 0)
        return pl.BlockSpec((None, None, heads, d, page), imap)

    grid_spec = pltpu.PrefetchScalarGridSpec(
        num_scalar_prefetch=1,
        grid=(nb, ng),
        in_specs=[pl.BlockSpec((1, heads, d, page), lambda b, g, pt: (b, 0, 0, 0)),
                  pl.BlockSpec((heads, 1), lambda b, g, pt: (0, 0)),
                  pl.BlockSpec(tri.shape, lambda b, g, pt: (0, 0))]
                 + [page_spec(i) for i in range(gp)] * 2,
        out_specs=pl.BlockSpec((1, heads, d), lambda b, g, pt: (b, 0, 0)),
        scratch_shapes=[pltpu.VMEM((heads, d, page), F32), pltpu.VMEM((heads, LANES), F32),
                        pltpu.VMEM((heads, page), F32)],
    )
    return pl.pallas_call(
        functools.partial(_sb_decode_kernel, n_pages=gp),
        grid_spec=grid_spec,
        out_shape=jax.ShapeDtypeStruct((nb, heads, d), F32),
        compiler_params=_params(2),
        name="sb_decode",
    )(page_table, qc, bias_col, tri, *([cache_k_t] * gp), *([cache_v_t] * gp))


def _merge_kernel(oa_ref, ob_ref, sga_ref, sgb_ref, x_ref, g1_ref, sh2_ref, sc2_ref,
                  wa_ref, wb_ref, wo_ref, gpm_ref, gpf_ref, wr_ref, br_ref,
                  x1_ref, h_ref, ti_ref, tp_ref):
    br_a = _dot(oa_ref[...], wa_ref[...])
    br_b = _dot(ob_ref[...], wb_ref[...])
    m = sga_ref[...].astype(F32) * br_a + sgb_ref[...].astype(F32) * br_b
    mixed = _dot(m.astype(BF16), wo_ref[...])
    x1 = x_ref[...] + g1_ref[0] * _rms(mixed, gpm_ref[...])
    x1_ref[...] = x1
    h = _rms(x1, gpf_ref[...]) * (1.0 + sc2_ref[0]) + sh2_ref[0]
    h_ref[...] = h

    logits = lax.dot_general(wr_ref[...], h, (((1,), (1,)), ((), ())), preferred_element_type=F32,
                             precision=lax.Precision.HIGHEST) + br_ref[...]
    eidx = lax.broadcasted_iota(I32, logits.shape, 0).astype(F32)
    vals = []
    for kk in range(TOP_K):
        mx = jnp.max(logits, axis=0, keepdims=True)
        sel = jnp.min(jnp.where(logits == mx, eidx, float(N_EXPERTS)), axis=0, keepdims=True)
        ti_ref[kk:kk + 1, :] = sel.astype(I32)
        vals.append(mx)
        logits = jnp.where(eidx == sel, -jnp.inf, logits)
    es = [jnp.exp(v - vals[0]) for v in vals]
    tot = es[0] + es[1] + es[2] + es[3]
    for kk in range(TOP_K):
        tp_ref[kk:kk + 1, :] = es[kk] / tot


def _merge_call(oa, ob, sga, sgb, x2d, gate1, shift2, scale2, w, tm, tiles_per_mod):
    n = x2d.shape[0]
    row = lambda i: (i, 0)
    colb = lambda i: (0, i)
    consts = [w["gla_o"], w["sb_o"], w["out"], w["g_post_mix"], w["g_pre_ffn"], w["router_t"], w["b_router"]]
    return pl.pallas_call(
        _merge_kernel,
        grid=(n // tm,),
        in_specs=[pl.BlockSpec((tm, GLA_VAL), row), pl.BlockSpec((tm, SB_WIDTH), row),
                  pl.BlockSpec((tm, D_MODEL), row), pl.BlockSpec((tm, D_MODEL), row),
                  pl.BlockSpec((tm, D_MODEL), row),
                  _mod_spec(gate1, tiles_per_mod), _mod_spec(shift2, tiles_per_mod),
                  _mod_spec(scale2, tiles_per_mod)] + [_const_spec(a) for a in consts],
        out_specs=[pl.BlockSpec((tm, D_MODEL), row), pl.BlockSpec((tm, D_MODEL), row),
                   pl.BlockSpec((TOP_K, tm), colb), pl.BlockSpec((TOP_K, tm), colb)],
        out_shape=[jax.ShapeDtypeStruct((n, D_MODEL), F32), jax.ShapeDtypeStruct((n, D_MODEL), F32),
                   jax.ShapeDtypeStruct((TOP_K, n), I32), jax.ShapeDtypeStruct((TOP_K, n), F32)],
        compiler_params=_params(1),
        name="merge",
    )(oa, ob, sga, sgb, x2d, gate1, shift2, scale2, *consts)


def _row_copy(src, src_row, dst, dst_row, sem):
    return pltpu.make_async_copy(src.at[pl.ds(src_row, 1), :], dst.at[pl.ds(dst_row, 1), :], sem)


def _dispatch_kernel(dest_ref, h_ref, xb_in_ref, xb_ref, idx_smem, idx_sem, row_sem):
    del xb_in_ref
    tm = h_ref.shape[0]
    idx_copy = pltpu.make_async_copy(dest_ref, idx_smem, idx_sem)
    idx_copy.start()
    idx_copy.wait()

    def body(t, carry):
        for kk in range(TOP_K):
            _row_copy(h_ref, t, xb_ref, idx_smem[kk, t], row_sem).start()
        return carry

    lax.fori_loop(0, tm, body, 0)
    for kk in range(TOP_K):
        pltpu.make_async_copy(h_ref, xb_ref.at[pl.ds(0, tm), :], row_sem).wait()


def _dispatch_call(dest, h, xb_zero, tm):
    n = h.shape[0]
    return pl.pallas_call(
        _dispatch_kernel,
        grid=(n // tm,),
        in_specs=[pl.BlockSpec((TOP_K, tm), lambda i: (0, i)),
                  pl.BlockSpec((tm, D_MODEL), lambda i: (i, 0)),
                  pl.BlockSpec(memory_space=pl.ANY)],
        out_specs=pl.BlockSpec(memory_space=pl.ANY),
        out_shape=jax.ShapeDtypeStruct(xb_zero.shape, F32),
        scratch_shapes=[pltpu.SMEM((TOP_K, tm), I32), pltpu.SemaphoreType.DMA, pltpu.SemaphoreType.DMA],
        input_output_aliases={2: 0},
        compiler_params=_params(1),
        name="dispatch",
    )(dest, h, xb_zero)


def _moe_kernel(be_ref, x_ref, wgu_ref, bgu_ref, wd_ref, bd_ref, y_ref):
    del be_ref
    x = x_ref[...].astype(BF16)
    gu = _dot(x, wgu_ref[0]) + bgu_ref[0]
    g = jnp.minimum(gu[:, :EXPERT_FF], SWIGLU_LIMIT)
    u = jnp.clip(gu[:, EXPERT_FF:], -SWIGLU_LIMIT, SWIGLU_LIMIT)
    act = (u + 1.0) * (g * _sigmoid(SWIGLU_ALPHA * g))
    y_ref[...] = _dot(act.astype(BF16), wd_ref[0]) + bd_ref[0]


def _moe_call(block_e, xb, w):
    r = xb.shape[0]
    bm = MOE_ROWS
    emap = lambda i, be: (be[i], 0, 0)
    grid_spec = pltpu.PrefetchScalarGridSpec(
        num_scalar_prefetch=1,
        grid=(r // bm,),
        in_specs=[pl.BlockSpec((bm, D_MODEL), lambda i, be: (i, 0)),
                  pl.BlockSpec((1, D_MODEL, 2 * EXPERT_FF), emap),
                  pl.BlockSpec((1, 1, 2 * EXPERT_FF), emap),
                  pl.BlockSpec((1, EXPERT_FF, D_MODEL), emap),
                  pl.BlockSpec((1, 1, D_MODEL), emap)],
        out_specs=pl.BlockSpec((bm, D_MODEL), lambda i, be: (i, 0)),
    )
    return pl.pallas_call(
        _moe_kernel,
        grid_spec=grid_spec,
        out_shape=jax.ShapeDtypeStruct((r, D_MODEL), F32),
        compiler_params=_params(1),
        name="moe",
    )(block_e, xb, w["gate_up"], w["b_gate_up"], w["down"], w["b_down"])


def _final_kernel(dest_ref, p_ref, x1_ref, g2_ref, gpf_ref, yb_ref, out_ref, idx_smem, ybuf, idx_sem, row_sem):
    tm = x1_ref.shape[0]
    idx_copy = pltpu.make_async_copy(dest_ref, idx_smem, idx_sem)
    idx_copy.start()
    idx_copy.wait()

    def body(t, carry):
        for kk in range(TOP_K):
            _row_copy(yb_ref, idx_smem[kk, t], ybuf.at[kk], t, row_sem).start()
        return carry

    lax.fori_loop(0, tm, body, 0)
    for kk in range(TOP_K):
        pltpu.make_async_copy(yb_ref.at[pl.ds(0, tm), :], ybuf.at[kk], row_sem).wait()

    p = p_ref[...]
    f = p[:, 0:1] * ybuf[0]
    for kk in range(1, TOP_K):
        f = f + p[:, kk:kk + 1] * ybuf[kk]
    out_ref[...] = x1_ref[...] + g2_ref[0] * _rms(f, gpf_ref[...])


def _final_call(dest, probs_t, x1, gate2, g_post_ffn, yb, tm, tiles_per_mod):
    n = x1.shape[0]
    return pl.pallas_call(
        _final_kernel,
        grid=(n // tm,),
        in_specs=[pl.BlockSpec((TOP_K, tm), lambda i: (0, i)),
                  pl.BlockSpec((tm, TOP_K), lambda i: (i, 0)),
                  pl.BlockSpec((tm, D_MODEL), lambda i: (i, 0)),
                  _mod_spec(gate2, tiles_per_mod), _const_spec(g_post_ffn),
                  pl.BlockSpec(memory_space=pl.ANY)],
        out_specs=pl.BlockSpec((tm, D_MODEL), lambda i: (i, 0)),
        out_shape=jax.ShapeDtypeStruct((n, D_MODEL), F32),
        scratch_shapes=[pltpu.SMEM((TOP_K, tm), I32), pltpu.VMEM((TOP_K, tm, D_MODEL), F32),
                        pltpu.SemaphoreType.DMA, pltpu.SemaphoreType.DMA],
        compiler_params=_params(1),
        name="final",
    )(dest, probs_t, x1, gate2, g_post_ffn, yb)


def _route(top_i, bm):
    k, n = top_i.shape
    nk = k * n
    ef = top_i.reshape(nk)
    oh = (ef[:, None] == jnp.arange(N_EXPERTS, dtype=I32)[None, :]).astype(I32)
    cs = jnp.cumsum(oh, axis=0)
    rank = jnp.sum((cs - oh) * oh, axis=1)
    counts = cs[-1]
    padded = (counts + bm - 1) // bm * bm
    pend = jnp.cumsum(padded)
    pstart = pend - padded
    dest = jnp.sum(oh * pstart[None, :], axis=1) + rank
    n_blocks = -(-nk // bm) + N_EXPERTS
    block_e = jnp.searchsorted(pend, jnp.arange(n_blocks, dtype=I32) * bm, side="right")
    block_e = jnp.minimum(block_e, N_EXPERTS - 1).astype(I32)
    return dest.reshape(k, n).astype(I32), block_e, n_blocks * bm


def _pick_tile(n, pref):
    t = pref
    while n % t:
        t //= 2
    return t


def _layer(x, ada, s0, w, sb_fn):
    batch, seq, _ = x.shape
    n = batch * seq
    x2d = x.reshape(n, D_MODEL)
    if seq == 1:
        tm = _pick_tile(n, 128)
        mods = [ada[:, i].reshape(n // tm, tm, D_MODEL) for i in range(N_ADA)]
        tiles_per_mod = 1
    else:
        tm = _pick_tile(seq, 256)
        mods = [ada[:, i].reshape(batch, 1, D_MODEL) for i in range(N_ADA)]
        tiles_per_mod = seq // tm
    shift1, scale1, gate1, shift2, scale2, gate2 = mods

    qa, ka, la, va, ra, qb, kb, vb, kf, vf, sga, sgb = _inproj_call(
        x2d, shift1, scale1, w["g_pre_mix"], w, tm, tiles_per_mod)

    seq_p = -(-seq // GLA_CHUNK) * GLA_CHUNK
    if seq_p != seq:
        def pad_t(a):
            a = a.reshape(batch, seq, a.shape[-1])
            return jnp.pad(a, ((0, 0), (0, seq_p - seq), (0, 0))).reshape(batch * seq_p, a.shape[-1])
        gq, gk, gl, gv, gr = [pad_t(a) for a in (qa, ka, la, va, ra)]
    else:
        gq, gk, gl, gv, gr = qa, ka, la, va, ra
    tt = _pick_tile(seq_p, 512)
    oa, s_new = _gla_call(gq, gk, gl, gv, gr, s0, w["g_gla_head"], batch, seq_p, tt)
    if seq_p != seq:
        oa = oa.reshape(batch, seq_p, GLA_VAL)[:, :seq].reshape(n, GLA_VAL)

    ob = sb_fn(qb, kb, vb)

    x1, h, top_i, top_p = _merge_call(oa, ob, sga, sgb, x2d, gate1, shift2, scale2, w, tm, tiles_per_mod)

    dest, block_e, n_rows = _route(top_i, MOE_ROWS)
    xb = _dispatch_call(dest, h, jnp.zeros((n_rows, D_MODEL), F32), tm)
    yb = _moe_call(block_e, xb, w)
    y = _final_call(dest, top_p.T, x1, gate2, w["g_post_ffn"], yb, tm, tiles_per_mod)

    kv_shape = (batch, seq, SB_HEADS, SB_HEAD_DIM)
    return y.reshape(batch, seq, D_MODEL), kf.reshape(kv_shape), vf.reshape(kv_shape), s_new


def _layer_weights(l, w_in, w_alpha, b_alpha, g_pre_mix, g_gla_head, w_gla_o, w_sb_o, w_out, g_post_mix,
                   g_pre_ffn, w_router, b_router, w_gate_up, b_gate_up, w_down, b_down, g_post_ffn):
    wi = w_in[l]
    o_lr = 2 * GLA_KEY + 2 * GLA_VAL
    o_sb = o_lr + GLA_LOWRANK
    o_gt = o_sb + 3 * SB_WIDTH
    w_lr = jnp.pad(wi[:, o_lr:o_sb], ((0, 0), (0, LANES - GLA_LOWRANK)))
    w_al = jnp.pad(w_alpha[l], ((0, LANES - GLA_LOWRANK), (0, 0)))
    row = lambda a: a[l].reshape(1, -1)
    return {
        "gla": wi[:, :o_lr].astype(BF16), "lr": w_lr.astype(BF16), "alpha": w_al.astype(BF16),
        "b_alpha": row(b_alpha), "sb": wi[:, o_sb:o_gt].astype(BF16), "gates": wi[:, o_gt:].astype(BF16),
        "g_pre_mix": row(g_pre_mix), "g_gla_head": row(g_gla_head),
        "gla_o": w_gla_o[l].astype(BF16), "sb_o": w_sb_o[l].astype(BF16), "out": w_out[l].astype(BF16),
        "g_post_mix": row(g_post_mix), "g_pre_ffn": row(g_pre_ffn),
        "router_t": w_router[l].T, "b_router": b_router[l].reshape(N_EXPERTS, 1),
        "gate_up": w_gate_up[l].astype(BF16), "b_gate_up": b_gate_up[l].reshape(N_EXPERTS, 1, -1),
        "down": w_down[l].astype(BF16), "b_down": b_down[l].reshape(N_EXPERTS, 1, -1),
        "g_post_ffn": row(g_post_ffn),
    }


def kernel(x_prompt, x_sample, cache_sb_k, cache_sb_v, state_gla, page_table, c_prompt, c_sample, w_ada, b_ada, g_pre_mix, w_in, w_alpha, b_alpha, g_gla_head, w_gla_o, b_sb_logit, w_sb_o, w_out, g_post_mix, g_pre_ffn, w_router, b_router, w_gate_up, b_gate_up, w_down, b_down, g_post_ffn):
    depth = w_in.shape[0]
    bp, sp, _ = x_prompt.shape
    bs, ss, _ = x_sample.shape
    assert ss == 1, "the sample group decodes one token per sequence"
    page = cache_sb_k.shape[2]
    ridx = jnp.arange(page, dtype=I32)
    tri = (ridx[:, None] > ridx[None, :]).astype(BF16)
    cache_k_t = jnp.transpose(cache_sb_k, (0, 1, 3, 4, 2))
    cache_v_t = jnp.transpose(cache_sb_v, (0, 1, 3, 4, 2))

    yp, ys = x_prompt, x_sample
    outs = [[] for _ in range(6)]
    for l in range(depth):
        w = _layer_weights(l, w_in, w_alpha, b_alpha, g_pre_mix, g_gla_head, w_gla_o, w_sb_o, w_out, g_post_mix,
                           g_pre_ffn, w_router, b_router, w_gate_up, b_gate_up, w_down, b_down, g_post_ffn)
        c_all = jnp.concatenate([c_prompt, c_sample], axis=0)
        ada = _ada_call(c_all, w_ada[l], b_ada[l].reshape(1, -1))
        ada_p = ada[:bp].reshape(bp, N_ADA, D_MODEL)
        ada_s = ada[bp:].reshape(bs, N_ADA, D_MODEL)
        bias = b_sb_logit[l]

        tq = _pick_tile(sp, 256)
        sb_prompt = lambda qb, kb, vb: _sb_prompt_call(qb, kb, vb, bias, bp, sp, tq)
        s0_p = jnp.zeros((bp, GLA_HEADS, GLA_HEAD_K, GLA_HEAD_V), F32)
        yp, kp, vp, stp = _layer(yp, ada_p, s0_p, w, sb_prompt)

        def sb_sample(qb, kb, vb):
            q3 = qb.reshape(bs, SB_HEADS, SB_HEAD_DIM).astype(F32)
            qc = jnp.broadcast_to(q3[..., None], q3.shape + (page,))
            o = _sb_decode_call(qc, bias.reshape(SB_HEADS, 1), tri, cache_k_t, cache_v_t, page_table, l)
            return o.reshape(bs, SB_WIDTH).astype(BF16)

        ys, kn, vn, stn = _layer(ys, ada_s, state_gla[l], w, sb_sample)
        for lst, val in zip(outs, (kp, vp, stp, kn, vn, stn)):
            lst.append(val)
    return (yp, ys) + tuple(jnp.stack(o) for o in outs)
```

```python
import functools

import jax
import jax.numpy as jnp
from jax import lax
from jax.experimental import pallas as pl
from jax.experimental.pallas import tpu as pltpu

F32 = jnp.float32
BF16 = jnp.bfloat16
I32 = jnp.int32

D_MODEL = 1024
GLA_HEADS = 4
GLA_HEAD_K = 64
GLA_HEAD_V = 128
GLA_KEY = GLA_HEADS * GLA_HEAD_K
GLA_VAL = GLA_HEADS * GLA_HEAD_V
GLA_LOWRANK = 16
GLA_TAU = 16.0
GLA_CHUNK = 64
SB_HEADS = 8
SB_HEAD_DIM = 64
SB_WIDTH = SB_HEADS * SB_HEAD_DIM
N_EXPERTS = 32
TOP_K = 4
EXPERT_FF = D_MODEL
SWIGLU_LIMIT = 7.0
SWIGLU_ALPHA = 1.702
RMS_EPS = 1e-6
N_ADA = 6

LOG2E = 1.4426950408889634
LANES = 128
VMEM_LIMIT = 56 * 1024 * 1024
MOE_ROWS = 256
DECODE_PAGES = 8


def _params(n_axes, vmem=VMEM_LIMIT):
    return pltpu.CompilerParams(dimension_semantics=("arbitrary",) * n_axes, vmem_limit_bytes=vmem)


def _sigmoid(x):
    return 1.0 / (1.0 + jnp.exp(-x))


def _softplus(z):
    return jnp.maximum(z, 0.0) + jnp.log(1.0 + jnp.exp(-jnp.abs(z)))


def _softplus2(z2):
    return jnp.maximum(z2, 0.0) + jnp.log2(1.0 + jnp.exp2(-jnp.abs(z2)))


def _rms(x, g):
    ms = jnp.mean(x * x, axis=-1, keepdims=True)
    return x * lax.rsqrt(ms + RMS_EPS) * g


def _dot(a, b):
    return jnp.dot(a, b, preferred_element_type=F32)


def _dot_nt(a, b):
    return lax.dot_general(a, b, (((1,), (1,)), ((), ())), preferred_element_type=F32)


def _dot_tn(a, b):
    return lax.dot_general(a, b, (((0,), (0,)), ((), ())), preferred_element_type=F32)


def _ada_kernel(c_ref, w_ref, b_ref, o_ref):
    c = c_ref[...]
    s = c * _sigmoid(c)
    o_ref[...] = jnp.dot(s, w_ref[...], preferred_element_type=F32,
                         precision=lax.Precision.HIGHEST) + b_ref[...]


def _ada_call(c_all, w_ada, b_ada):
    rows = c_all.shape[0]
    n_out = w_ada.shape[1]
    tn = D_MODEL
    return pl.pallas_call(
        _ada_kernel,
        grid=(n_out // tn,),
        in_specs=[pl.BlockSpec((rows, D_MODEL), lambda j: (0, 0)),
                  pl.BlockSpec((D_MODEL, tn), lambda j: (0, j)),
                  pl.BlockSpec((1, tn), lambda j: (0, j))],
        out_specs=pl.BlockSpec((rows, tn), lambda j: (0, j)),
        out_shape=jax.ShapeDtypeStruct((rows, n_out), F32),
        compiler_params=_params(1),
        name="ada",
    )(c_all, w_ada, b_ada)


def _inproj_kernel(x_ref, sh_ref, sc_ref, g_ref, wg_ref, wlr_ref, wal_ref, bal_ref, wsb_ref, wgt_ref,
                   qa_ref, ka_ref, la_ref, va_ref, ra_ref, qb_ref, kb_ref, vb_ref, kf_ref, vf_ref,
                   sga_ref, sgb_ref, *, kv_transposed):
    x = x_ref[...]
    u = _rms(x, g_ref[...]) * (1.0 + sc_ref[0]) + sh_ref[0]
    ub = u.astype(BF16)

    pg = _dot(ub, wg_ref[...])
    qa_ref[...] = pg[:, :GLA_KEY] * (GLA_HEAD_K ** -0.5)
    ka_ref[...] = pg[:, GLA_KEY:2 * GLA_KEY]
    va_ref[...] = pg[:, 2 * GLA_KEY:2 * GLA_KEY + GLA_VAL].astype(BF16)
    r = pg[:, 2 * GLA_KEY + GLA_VAL:]
    ra_ref[...] = (r * _sigmoid(r)).astype(BF16)

    lr = _dot(ub, wlr_ref[...])
    al = _dot(lr.astype(BF16), wal_ref[...]) + bal_ref[...]
    la_ref[...] = -_softplus(-al) * (1.0 / GLA_TAU)

    ps = _dot(ub, wsb_ref[...])
    qb_ref[...] = (ps[:, :SB_WIDTH] * (LOG2E * SB_HEAD_DIM ** -0.5)).astype(BF16)
    k = ps[:, SB_WIDTH:2 * SB_WIDTH]
    v = ps[:, 2 * SB_WIDTH:]
    if kv_transposed:
        kf_ref[0] = k.T.reshape(SB_HEADS, SB_HEAD_DIM, k.shape[0])
        vf_ref[0] = v.T.reshape(SB_HEADS, SB_HEAD_DIM, v.shape[0])
    else:
        kf_ref[...] = k
        vf_ref[...] = v
    kb_ref[...] = k.astype(BF16)
    vb_ref[...] = v.astype(BF16)

    pt = _dot(ub, wgt_ref[...])
    sga_ref[...] = _sigmoid(pt[:, :D_MODEL]).astype(BF16)
    sgb_ref[...] = _sigmoid(pt[:, D_MODEL:]).astype(BF16)


def _mod_spec(mod, tiles_per_mod):
    return pl.BlockSpec((1,) + mod.shape[1:], lambda i: (i // tiles_per_mod, 0, 0))


def _const_spec(a):
    nd = a.ndim
    return pl.BlockSpec(a.shape, lambda *_: (0,) * nd)


def _inproj_call(x2d, shift, scale, gain, w, tm, tiles_per_mod, kv_batch):
    n = x2d.shape[0]
    widths = [(GLA_KEY, F32), (GLA_KEY, F32), (GLA_KEY, F32), (GLA_VAL, BF16), (GLA_VAL, BF16),
              (SB_WIDTH, BF16), (SB_WIDTH, BF16), (SB_WIDTH, BF16), (SB_WIDTH, F32), (SB_WIDTH, F32),
              (D_MODEL, BF16), (D_MODEL, BF16)]
    row = lambda i: (i, 0)
    out_specs = [pl.BlockSpec((tm, wd), row) for wd, _ in widths]
    out_shape = [jax.ShapeDtypeStruct((n, wd), dt) for wd, dt in widths]
    if kv_batch is not None:
        tiles = n // kv_batch // tm
        for idx in (8, 9):
            out_specs[idx] = pl.BlockSpec((1, SB_HEADS, SB_HEAD_DIM, tm), lambda i: (i // tiles, 0, 0, i % tiles))
            out_shape[idx] = jax.ShapeDtypeStruct((kv_batch, SB_HEADS, SB_HEAD_DIM, n // kv_batch), F32)
    consts = [gain, w["gla"], w["lr"], w["alpha"], w["b_alpha"], w["sb"], w["gates"]]
    return pl.pallas_call(
        functools.partial(_inproj_kernel, kv_transposed=kv_batch is not None),
        grid=(n // tm,),
        in_specs=[pl.BlockSpec((tm, D_MODEL), row), _mod_spec(shift, tiles_per_mod),
                  _mod_spec(scale, tiles_per_mod)] + [_const_spec(a) for a in consts],
        out_specs=out_specs,
        out_shape=out_shape,
        compiler_params=_params(1),
        name="inproj",
    )(x2d, shift, scale, *consts)


def _gla_kernel(q_ref, k_ref, la_ref, v_ref, r_ref, s0_ref, g_ref, o_ref, sn_ref, st_sc, *, n_chunks):
    j = pl.program_id(1)
    C = GLA_CHUNK

    @pl.when(j == 0)
    def _():
        for p in range(2):
            s_pair = jnp.concatenate([s0_ref[0, 2 * p], s0_ref[0, 2 * p + 1]], axis=0)
            st_sc[p] = s_pair.T

    row = lax.broadcasted_iota(I32, (C, C), 0)
    col = lax.broadcasted_iota(I32, (C, C), 1)
    causal = col <= row
    l_incl = jnp.where(causal, 1.0, 0.0).astype(BF16)
    lane = lax.broadcasted_iota(I32, (C, LANES), 1)
    lane_sq = lax.broadcasted_iota(I32, (LANES, LANES), 1)

    def chunk(c, carry):
        sl = pl.ds(pl.multiple_of(c * C, C), C)
        a = la_ref[sl, :]
        a_hi = a.astype(BF16)
        a_lo = (a - a_hi.astype(F32)).astype(BF16)
        b = _dot(l_incl, a_hi) + _dot(l_incl, a_lo)
        b_last = b[C - 1:C, :]
        q_t = q_ref[sl, :] * jnp.exp(b)
        k = k_ref[sl, :]
        k_t = (k * jnp.exp(-b)).astype(BF16)
        k_u = (k * jnp.exp(b_last - b)).astype(BF16)
        dec = jnp.exp(b_last)
        for p in range(2):
            ps = slice(LANES * p, LANES * (p + 1))
            qp = q_t[:, ps]
            kp = k_t[:, ps]
            kup = k_u[:, ps]
            st = st_sc[p]
            stb = st.astype(BF16)
            upd = []
            for hh in range(2):
                h = 2 * p + hh
                hs = slice(GLA_HEAD_V * h, GLA_HEAD_V * (h + 1))
                in_head = (lane >= GLA_HEAD_K * hh) & (lane < GLA_HEAD_K * (hh + 1))
                qm = jnp.where(in_head, qp, 0.0).astype(BF16)
                s = jnp.where(causal, _dot_nt(qm, kp), 0.0)
                vh = v_ref[sl, hs]
                o = _dot_nt(qm, stb) + _dot(s.astype(BF16), vh)
                on = _rms(o, g_ref[...])
                o_ref[sl, hs] = (on * r_ref[sl, hs].astype(F32)).astype(BF16)
                upd.append(_dot_tn(vh, kup))
            st_sc[p] = st * dec[:, ps] + jnp.where(lane_sq < GLA_HEAD_K, upd[0], upd[1])
        return carry

    lax.fori_loop(0, n_chunks, chunk, 0)

    @pl.when(j == pl.num_programs(1) - 1)
    def _():
        for p in range(2):
            s_pair = st_sc[p].T
            sn_ref[0, 2 * p] = s_pair[:GLA_HEAD_K]
            sn_ref[0, 2 * p + 1] = s_pair[GLA_HEAD_K:]


def _gla_call(qa, ka, la, va, ra, s0, g_head, batch, seq, tt):
    n = qa.shape[0]
    nt = seq // tt
    rowmap = lambda b, j: (b * nt + j, 0)
    smap = lambda b, j: (b, 0, 0, 0)
    sblock = (1, GLA_HEADS, GLA_HEAD_K, GLA_HEAD_V)
    return pl.pallas_call(
        functools.partial(_gla_kernel, n_chunks=tt // GLA_CHUNK),
        grid=(batch, nt),
        in_specs=[pl.BlockSpec((tt, GLA_KEY), rowmap), pl.BlockSpec((tt, GLA_KEY), rowmap),
                  pl.BlockSpec((tt, GLA_KEY), rowmap), pl.BlockSpec((tt, GLA_VAL), rowmap),
                  pl.BlockSpec((tt, GLA_VAL), rowmap), pl.BlockSpec(sblock, smap),
                  pl.BlockSpec((1, GLA_HEAD_V), lambda b, j: (0, 0))],
        out_specs=[pl.BlockSpec((tt, GLA_VAL), rowmap), pl.BlockSpec(sblock, smap)],
        out_shape=[jax.ShapeDtypeStruct((n, GLA_VAL), BF16),
                   jax.ShapeDtypeStruct((batch,) + sblock[1:], F32)],
        scratch_shapes=[pltpu.VMEM((2, LANES, LANES), F32)],
        compiler_params=_params(2),
        name="gla",
    )(qa, ka, la, va, ra, s0, g_head)


def _sb_prompt_kernel(bias_ref, q_ref, k_ref, v_ref, o_ref, *, tq):
    pair = pl.program_id(1)
    qi = pl.program_id(2)
    q2 = q_ref[...]
    lane = lax.broadcasted_iota(I32, (tq, LANES), 1)
    row = lax.broadcasted_iota(I32, (tq, tq), 0)
    col = lax.broadcasted_iota(I32, (tq, tq), 1)
    strict = row < col
    later = jnp.where(col > row, 1.0, 0.0).astype(BF16)
    qms = []
    for p in range(2):
        in_head = (lane >= SB_HEAD_DIM * p) & (lane < SB_HEAD_DIM * (p + 1))
        qms.append(jnp.where(in_head, q2, jnp.zeros_like(q2)))
    biases = [bias_ref[2 * pair], bias_ref[2 * pair + 1]]

    def key_blocks(js, masked, carry):
        kv = []
        for j in js:
            ks = pl.ds(pl.multiple_of(j * tq, tq), tq)
            kv.append((k_ref[ks, :], v_ref[ks, :]))
        zs, sps, cums = [], [], []
        for kblk, _ in kv:
            for p in range(2):
                z = _dot_nt(kblk, qms[p]) + biases[p]
                sp = _softplus2(z)
                if masked:
                    sp = jnp.where(strict, sp, 0.0)
                zs.append(z)
                sps.append(sp)
        for sp in sps:
            cums.append(_dot(later, sp.astype(BF16)))
        accs = [carry[0], carry[2]]
        runs = [carry[1], carry[3]]
        for b, (_, vblk) in enumerate(kv):
            for p in range(2):
                i = 2 * b + p
                cum = cums[i] + runs[p]
                a = jnp.exp2(zs[i] - sps[i] - cum)
                if masked:
                    a = jnp.where(strict, a, 0.0)
                accs[p] = accs[p] + _dot_tn(vblk, a.astype(BF16))
                runs[p] = cum[0:1, :] + sps[i][0:1, :]
        return accs[0], runs[0], accs[1], runs[1]

    zacc = jnp.zeros((LANES, tq), F32)
    zrun = jnp.zeros((1, tq), F32)
    carry = key_blocks([qi], True, (zacc, zrun, zacc, zrun))
    odd = qi % 2
    carry = lax.fori_loop(0, odd, lambda _, c: key_blocks([qi - 1], False, c), carry)

    def two_blocks(jj, c):
        j_hi = qi - odd - 1 - 2 * jj
        return key_blocks([j_hi, j_hi - 1], False, c)

    carry = lax.fori_loop(0, qi // 2, two_blocks, carry)
    rows = lax.broadcasted_iota(I32, (LANES, tq), 0)
    o_t = jnp.where(rows < SB_HEAD_DIM, carry[0], carry[2])
    o_ref[...] = o_t.T.astype(BF16)


def _sb_prompt_call(qb, kb, vb, bias, batch, seq, tq):
    n = qb.shape[0]
    nq = seq // tq
    return pl.pallas_call(
        functools.partial(_sb_prompt_kernel, tq=tq),
        grid=(batch, SB_HEADS // 2, nq),
        in_specs=[pl.BlockSpec(memory_space=pltpu.SMEM),
                  pl.BlockSpec((tq, LANES), lambda b, p, i: (b * nq + i, p)),
                  pl.BlockSpec((seq, LANES), lambda b, p, i: (b, p)),
                  pl.BlockSpec((seq, LANES), lambda b, p, i: (b, p))],
        out_specs=pl.BlockSpec((tq, LANES), lambda b, p, i: (b * nq + i, p)),
        out_shape=jax.ShapeDtypeStruct((n, SB_WIDTH), BF16),
        compiler_params=_params(3),
        name="sb_prompt",
    )(bias, qb, kb, vb)


def _sb_decode_kernel(pt_ref, qc_ref, bias_ref, tri_ref, *refs, n_pages):
    k_refs = refs[:n_pages]
    v_refs = refs[n_pages:2 * n_pages]
    o_ref, acc_sc, run_sc, z_sc = refs[2 * n_pages:]
    g = pl.program_id(1)
    heads, d, page = k_refs[0].shape

    @pl.when(g == 0)
    def _():
        acc_sc[...] = jnp.zeros_like(acc_sc)
        run_sc[...] = jnp.zeros_like(run_sc)

    for i in range(n_pages):
        for h in range(heads):
            prod = k_refs[i][h] * qc_ref[0, h]
            r = i * heads + h
            z_sc[r:r + 1, :] = jnp.sum(prod, axis=0, keepdims=True)
    z = z_sc[...] + bias_ref[...]
    sp = _softplus2(z)
    cum = _dot(sp.astype(BF16), tri_ref[...])
    tot = jnp.sum(sp, axis=1, keepdims=True)
    lw = z - sp - cum
    run = run_sc[:, 0:1]
    weights = []
    for i in range(n_pages):
        rows = slice(i * heads, (i + 1) * heads)
        weights.append(jnp.exp2(lw[rows] - run))
        run = run + tot[rows]
    run_sc[...] = jnp.broadcast_to(run, run_sc.shape)
    for h in range(heads):
        acc = acc_sc[h]
        for i in range(n_pages):
            acc = acc + v_refs[i][h] * weights[i][h:h + 1, :]
        acc_sc[h] = acc

    @pl.when(g == pl.num_programs(1) - 1)
    def _():
        ones = jnp.ones((8, page), F32)
        for h in range(heads):
            tot = lax.dot_general(ones, acc_sc[h], (((1,), (1,)), ((), ())), preferred_element_type=F32,
                                  precision=lax.Precision.HIGHEST)
            o_ref[0, h:h + 1, :] = tot[0:1, :]


def _sb_decode_call(qc, bias, tri, cache_k_t, cache_v_t, page_table, layer):
    nb, heads, d, page = qc.shape
    n_seq_pages = page_table.shape[1]
    gp = DECODE_PAGES
    while n_seq_pages % gp:
        gp //= 2
    ng = n_seq_pages // gp
    bias_col = jnp.tile(bias.reshape(heads, 1), (gp, 1))

    def page_spec(i):
        def imap(b, g, pt):
            return (layer, pt[b, n_seq_pages - 1 - (g * gp + i)], 0, 0, 0)
        return pl.BlockSpec((None, None, heads, d, page), imap)

    grid_spec = pltpu.PrefetchScalarGridSpec(
        num_scalar_prefetch=1,
        grid=(nb, ng),
        in_specs=[pl.BlockSpec((1, heads, d, page), lambda b, g, pt: (b, 0, 0, 0)),
                  pl.BlockSpec((gp * heads, 1), lambda b, g, pt: (0, 0)),
                  pl.BlockSpec(tri.shape, lambda b, g, pt: (0, 0))]
                 + [page_spec(i) for i in range(gp)] * 2,
        out_specs=pl.BlockSpec((1, heads, d), lambda b, g, pt: (b, 0, 0)),
        scratch_shapes=[pltpu.VMEM((heads, d, page), F32), pltpu.VMEM((heads, LANES), F32),
                        pltpu.VMEM((gp * heads, page), F32)],
    )
    return pl.pallas_call(
        functools.partial(_sb_decode_kernel, n_pages=gp),
        grid_spec=grid_spec,
        out_shape=jax.ShapeDtypeStruct((nb, heads, d), F32),
        compiler_params=_params(2),
        name="sb_decode",
    )(page_table, qc, bias_col, tri, *([cache_k_t] * gp), *([cache_v_t] * gp))


def _merge_kernel(oa_ref, ob_ref, sga_ref, sgb_ref, x_ref, g1_ref, sh2_ref, sc2_ref,
                  wa_ref, wb_ref, wo_ref, gpm_ref, gpf_ref, wr_ref, br_ref,
                  x1_ref, h_ref, ti_ref, tp_ref):
    br_a = _dot(oa_ref[...], wa_ref[...])
    br_b = _dot(ob_ref[...], wb_ref[...])
    m = sga_ref[...].astype(F32) * br_a + sgb_ref[...].astype(F32) * br_b
    mixed = _dot(m.astype(BF16), wo_ref[...])
    x1 = x_ref[...] + g1_ref[0] * _rms(mixed, gpm_ref[...])
    x1_ref[...] = x1
    h = _rms(x1, gpf_ref[...]) * (1.0 + sc2_ref[0]) + sh2_ref[0]
    h_ref[...] = h

    logits = lax.dot_general(wr_ref[...], h, (((1,), (1,)), ((), ())), preferred_element_type=F32,
                             precision=lax.Precision.HIGHEST) + br_ref[...]
    eidx = lax.broadcasted_iota(I32, logits.shape, 0).astype(F32)
    vals = []
    for kk in range(TOP_K):
        mx = jnp.max(logits, axis=0, keepdims=True)
        sel = jnp.min(jnp.where(logits == mx, eidx, float(N_EXPERTS)), axis=0, keepdims=True)
        ti_ref[kk:kk + 1, :] = sel.astype(I32)
        vals.append(mx)
        logits = jnp.where(eidx == sel, -jnp.inf, logits)
    es = [jnp.exp(v - vals[0]) for v in vals]
    tot = es[0] + es[1] + es[2] + es[3]
    for kk in range(TOP_K):
        tp_ref[kk:kk + 1, :] = es[kk] / tot


def _merge_call(oa, ob, sga, sgb, x2d, gate1, shift2, scale2, w, tm, tiles_per_mod):
    n = x2d.shape[0]
    row = lambda i: (i, 0)
    colb = lambda i: (0, i)
    consts = [w["gla_o"], w["sb_o"], w["out"], w["g_post_mix"], w["g_pre_ffn"], w["router_t"], w["b_router"]]
    return pl.pallas_call(
        _merge_kernel,
        grid=(n // tm,),
        in_specs=[pl.BlockSpec((tm, GLA_VAL), row), pl.BlockSpec((tm, SB_WIDTH), row),
                  pl.BlockSpec((tm, D_MODEL), row), pl.BlockSpec((tm, D_MODEL), row),
                  pl.BlockSpec((tm, D_MODEL), row),
                  _mod_spec(gate1, tiles_per_mod), _mod_spec(shift2, tiles_per_mod),
                  _mod_spec(scale2, tiles_per_mod)] + [_const_spec(a) for a in consts],
        out_specs=[pl.BlockSpec((tm, D_MODEL), row), pl.BlockSpec((tm, D_MODEL), row),
                   pl.BlockSpec((TOP_K, tm), colb), pl.BlockSpec((TOP_K, tm), colb)],
        out_shape=[jax.ShapeDtypeStruct((n, D_MODEL), F32), jax.ShapeDtypeStruct((n, D_MODEL), F32),
                   jax.ShapeDtypeStruct((TOP_K, n), I32), jax.ShapeDtypeStruct((TOP_K, n), F32)],
        compiler_params=_params(1),
        name="merge",
    )(oa, ob, sga, sgb, x2d, gate1, shift2, scale2, *consts)


def _row_copy(src, src_row, dst, dst_row, sem):
    return pltpu.make_async_copy(src.at[pl.ds(src_row, 1), :], dst.at[pl.ds(dst_row, 1), :], sem)


def _dispatch_kernel(dest_ref, h_ref, xb_in_ref, xb_ref, idx_smem, idx_sem, row_sem):
    del xb_in_ref
    tm = h_ref.shape[0]
    idx_copy = pltpu.make_async_copy(dest_ref, idx_smem, idx_sem)
    idx_copy.start()
    idx_copy.wait()

    def body(t, carry):
        for kk in range(TOP_K):
            _row_copy(h_ref, t, xb_ref, idx_smem[kk, t], row_sem).start()
        return carry

    lax.fori_loop(0, tm, body, 0)
    for kk in range(TOP_K):
        pltpu.make_async_copy(h_ref, xb_ref.at[pl.ds(0, tm), :], row_sem).wait()


def _dispatch_call(dest, h, xb_zero, tm):
    n = h.shape[0]
    return pl.pallas_call(
        _dispatch_kernel,
        grid=(n // tm,),
        in_specs=[pl.BlockSpec((TOP_K, tm), lambda i: (0, i)),
                  pl.BlockSpec((tm, D_MODEL), lambda i: (i, 0)),
                  pl.BlockSpec(memory_space=pl.ANY)],
        out_specs=pl.BlockSpec(memory_space=pl.ANY),
        out_shape=jax.ShapeDtypeStruct(xb_zero.shape, F32),
        scratch_shapes=[pltpu.SMEM((TOP_K, tm), I32), pltpu.SemaphoreType.DMA, pltpu.SemaphoreType.DMA],
        input_output_aliases={2: 0},
        compiler_params=_params(1),
        name="dispatch",
    )(dest, h, xb_zero)


def _moe_kernel(be_ref, x_ref, wgu_ref, bgu_ref, wd_ref, bd_ref, y_ref):
    del be_ref
    x = x_ref[...].astype(BF16)
    gu = _dot(x, wgu_ref[0]) + bgu_ref[0]
    g = jnp.minimum(gu[:, :EXPERT_FF], SWIGLU_LIMIT)
    u = jnp.clip(gu[:, EXPERT_FF:], -SWIGLU_LIMIT, SWIGLU_LIMIT)
    act = (u + 1.0) * (g * _sigmoid(SWIGLU_ALPHA * g))
    y_ref[...] = _dot(act.astype(BF16), wd_ref[0]) + bd_ref[0]


def _moe_call(block_e, xb, w):
    r = xb.shape[0]
    bm = MOE_ROWS
    emap = lambda i, be: (be[i], 0, 0)
    grid_spec = pltpu.PrefetchScalarGridSpec(
        num_scalar_prefetch=1,
        grid=(r // bm,),
        in_specs=[pl.BlockSpec((bm, D_MODEL), lambda i, be: (i, 0)),
                  pl.BlockSpec((1, D_MODEL, 2 * EXPERT_FF), emap),
                  pl.BlockSpec((1, 1, 2 * EXPERT_FF), emap),
                  pl.BlockSpec((1, EXPERT_FF, D_MODEL), emap),
                  pl.BlockSpec((1, 1, D_MODEL), emap)],
        out_specs=pl.BlockSpec((bm, D_MODEL), lambda i, be: (i, 0)),
    )
    return pl.pallas_call(
        _moe_kernel,
        grid_spec=grid_spec,
        out_shape=jax.ShapeDtypeStruct((r, D_MODEL), F32),
        compiler_params=_params(1),
        name="moe",
    )(block_e, xb, w["gate_up"], w["b_gate_up"], w["down"], w["b_down"])


def _final_kernel(dest_ref, p_ref, x1_ref, g2_ref, gpf_ref, yb_ref, out_ref, idx_smem, ybuf, idx_sem, row_sem):
    tm = x1_ref.shape[0]
    idx_copy = pltpu.make_async_copy(dest_ref, idx_smem, idx_sem)
    idx_copy.start()
    idx_copy.wait()

    def body(t, carry):
        for kk in range(TOP_K):
            _row_copy(yb_ref, idx_smem[kk, t], ybuf.at[kk], t, row_sem).start()
        return carry

    lax.fori_loop(0, tm, body, 0)
    for kk in range(TOP_K):
        pltpu.make_async_copy(yb_ref.at[pl.ds(0, tm), :], ybuf.at[kk], row_sem).wait()

    p = p_ref[...]
    f = p[:, 0:1] * ybuf[0]
    for kk in range(1, TOP_K):
        f = f + p[:, kk:kk + 1] * ybuf[kk]
    out_ref[...] = x1_ref[...] + g2_ref[0] * _rms(f, gpf_ref[...])


def _final_call(dest, probs_t, x1, gate2, g_post_ffn, yb, tm, tiles_per_mod):
    n = x1.shape[0]
    return pl.pallas_call(
        _final_kernel,
        grid=(n // tm,),
        in_specs=[pl.BlockSpec((TOP_K, tm), lambda i: (0, i)),
                  pl.BlockSpec((tm, TOP_K), lambda i: (i, 0)),
                  pl.BlockSpec((tm, D_MODEL), lambda i: (i, 0)),
                  _mod_spec(gate2, tiles_per_mod), _const_spec(g_post_ffn),
                  pl.BlockSpec(memory_space=pl.ANY)],
        out_specs=pl.BlockSpec((tm, D_MODEL), lambda i: (i, 0)),
        out_shape=jax.ShapeDtypeStruct((n, D_MODEL), F32),
        scratch_shapes=[pltpu.SMEM((TOP_K, tm), I32), pltpu.VMEM((TOP_K, tm, D_MODEL), F32),
                        pltpu.SemaphoreType.DMA, pltpu.SemaphoreType.DMA],
        compiler_params=_params(1),
        name="final",
    )(dest, probs_t, x1, gate2, g_post_ffn, yb)


def _route(top_i, bm):
    k, n = top_i.shape
    nk = k * n
    ef = top_i.reshape(nk)
    oh = (ef[:, None] == jnp.arange(N_EXPERTS, dtype=I32)[None, :]).astype(I32)
    cs = jnp.cumsum(oh, axis=0)
    rank = jnp.sum((cs - oh) * oh, axis=1)
    counts = cs[-1]
    padded = (counts + bm - 1) // bm * bm
    pend = jnp.cumsum(padded)
    pstart = pend - padded
    dest = jnp.sum(oh * pstart[None, :], axis=1) + rank
    n_blocks = -(-nk // bm) + N_EXPERTS
    first_row = jnp.arange(n_blocks, dtype=I32) * bm
    block_e = jnp.sum((pend[None, :] <= first_row[:, None]).astype(I32), axis=1)
    block_e = jnp.minimum(block_e, N_EXPERTS - 1).astype(I32)
    return dest.reshape(k, n).astype(I32), block_e, n_blocks * bm


def _pick_tile(n, pref):
    t = pref
    while n % t:
        t //= 2
    return t


def _layer(x, ada, s0, w, sb_fn):
    batch, seq, _ = x.shape
    n = batch * seq
    x2d = x.reshape(n, D_MODEL)
    if seq == 1:
        tm = _pick_tile(n, 128)
        mods = [ada[:, i].reshape(n // tm, tm, D_MODEL) for i in range(N_ADA)]
        tiles_per_mod = 1
    else:
        tm = _pick_tile(seq, 256)
        mods = [ada[:, i].reshape(batch, 1, D_MODEL) for i in range(N_ADA)]
        tiles_per_mod = seq // tm
    shift1, scale1, gate1, shift2, scale2, gate2 = mods

    kv_batch = batch if seq % LANES == 0 else None
    qa, ka, la, va, ra, qb, kb, vb, kf, vf, sga, sgb = _inproj_call(
        x2d, shift1, scale1, w["g_pre_mix"], w, tm, tiles_per_mod, kv_batch)

    seq_p = -(-seq // GLA_CHUNK) * GLA_CHUNK
    if seq_p != seq:
        def pad_t(a):
            a = a.reshape(batch, seq, a.shape[-1])
            return jnp.pad(a, ((0, 0), (0, seq_p - seq), (0, 0))).reshape(batch * seq_p, a.shape[-1])
        gq, gk, gl, gv, gr = [pad_t(a) for a in (qa, ka, la, va, ra)]
    else:
        gq, gk, gl, gv, gr = qa, ka, la, va, ra
    tt = _pick_tile(seq_p, 512)
    oa, s_new = _gla_call(gq, gk, gl, gv, gr, s0, w["g_gla_head"], batch, seq_p, tt)
    if seq_p != seq:
        oa = oa.reshape(batch, seq_p, GLA_VAL)[:, :seq].reshape(n, GLA_VAL)

    ob = sb_fn(qb, kb, vb)

    x1, h, top_i, top_p = _merge_call(oa, ob, sga, sgb, x2d, gate1, shift2, scale2, w, tm, tiles_per_mod)

    dest, block_e, n_rows = _route(top_i, MOE_ROWS)
    xb = _dispatch_call(dest, h, jnp.zeros((n_rows, D_MODEL), F32), tm)
    yb = _moe_call(block_e, xb, w)
    y = _final_call(dest, top_p.T, x1, gate2, w["g_post_ffn"], yb, tm, tiles_per_mod)

    if kv_batch is not None:
        kf, vf = [jnp.transpose(a, (0, 3, 1, 2)) for a in (kf, vf)]
    kv_shape = (batch, seq, SB_HEADS, SB_HEAD_DIM)
    return y.reshape(batch, seq, D_MODEL), kf.reshape(kv_shape), vf.reshape(kv_shape), s_new


def _layer_weights(l, w_in, w_alpha, b_alpha, g_pre_mix, g_gla_head, w_gla_o, w_sb_o, w_out, g_post_mix,
                   g_pre_ffn, w_router, b_router, w_gate_up, b_gate_up, w_down, b_down, g_post_ffn):
    wi = w_in[l]
    o_lr = 2 * GLA_KEY + 2 * GLA_VAL
    o_sb = o_lr + GLA_LOWRANK
    o_gt = o_sb + 3 * SB_WIDTH
    w_lr = jnp.pad(wi[:, o_lr:o_sb], ((0, 0), (0, LANES - GLA_LOWRANK)))
    w_al = jnp.pad(w_alpha[l], ((0, LANES - GLA_LOWRANK), (0, 0)))
    row = lambda a: a[l].reshape(1, -1)
    return {
        "gla": wi[:, :o_lr].astype(BF16), "lr": w_lr.astype(BF16), "alpha": w_al.astype(BF16),
        "b_alpha": row(b_alpha), "sb": wi[:, o_sb:o_gt].astype(BF16), "gates": wi[:, o_gt:].astype(BF16),
        "g_pre_mix": row(g_pre_mix), "g_gla_head": row(g_gla_head),
        "gla_o": w_gla_o[l].astype(BF16), "sb_o": w_sb_o[l].astype(BF16), "out": w_out[l].astype(BF16),
        "g_post_mix": row(g_post_mix), "g_pre_ffn": row(g_pre_ffn),
        "router_t": w_router[l].T, "b_router": b_router[l].reshape(N_EXPERTS, 1),
        "gate_up": w_gate_up[l].astype(BF16), "b_gate_up": b_gate_up[l].reshape(N_EXPERTS, 1, -1),
        "down": w_down[l].astype(BF16), "b_down": b_down[l].reshape(N_EXPERTS, 1, -1),
        "g_post_ffn": row(g_post_ffn),
    }


def kernel(x_prompt, x_sample, cache_sb_k, cache_sb_v, state_gla, page_table, c_prompt, c_sample, w_ada, b_ada, g_pre_mix, w_in, w_alpha, b_alpha, g_gla_head, w_gla_o, b_sb_logit, w_sb_o, w_out, g_post_mix, g_pre_ffn, w_router, b_router, w_gate_up, b_gate_up, w_down, b_down, g_post_ffn):
    depth = w_in.shape[0]
    bp, sp, _ = x_prompt.shape
    bs, ss, _ = x_sample.shape
    assert ss == 1, "the sample group decodes one token per sequence"
    page = cache_sb_k.shape[2]
    ridx = jnp.arange(page, dtype=I32)
    tri = (ridx[:, None] > ridx[None, :]).astype(BF16)
    cache_k_t = jnp.transpose(cache_sb_k, (0, 1, 3, 4, 2))
    cache_v_t = jnp.transpose(cache_sb_v, (0, 1, 3, 4, 2))

    yp, ys = x_prompt, x_sample
    outs = [[] for _ in range(6)]
    for l in range(depth):
        w = _layer_weights(l, w_in, w_alpha, b_alpha, g_pre_mix, g_gla_head, w_gla_o, w_sb_o, w_out, g_post_mix,
                           g_pre_ffn, w_router, b_router, w_gate_up, b_gate_up, w_down, b_down, g_post_ffn)
        c_all = jnp.concatenate([c_prompt, c_sample], axis=0)
        ada = _ada_call(c_all, w_ada[l], b_ada[l].reshape(1, -1))
        ada_p = ada[:bp].reshape(bp, N_ADA, D_MODEL)
        ada_s = ada[bp:].reshape(bs, N_ADA, D_MODEL)
        bias = b_sb_logit[l] * LOG2E

        tq = _pick_tile(sp, 256)
        sb_prompt = lambda qb, kb, vb: _sb_prompt_call(qb, kb, vb, bias, bp, sp, tq)
        s0_p = jnp.zeros((bp, GLA_HEADS, GLA_HEAD_K, GLA_HEAD_V), F32)
        yp, kp, vp, stp = _layer(yp, ada_p, s0_p, w, sb_prompt)

        def sb_sample(qb, kb, vb):
            q3 = qb.reshape(bs, SB_HEADS, SB_HEAD_DIM).astype(F32)
            qc = jnp.broadcast_to(q3[..., None], q3.shape + (page,))
            o = _sb_decode_call(qc, bias, tri, cache_k_t, cache_v_t, page_table, l)
            return o.reshape(bs, SB_WIDTH).astype(BF16)

        ys, kn, vn, stn = _layer(ys, ada_s, state_gla[l], w, sb_sample)
        for lst, val in zip(outs, (kp, vp, stp, kn, vn, stn)):
            lst.append(val)
    return (yp, ys) + tuple(jnp.stack(o) for o in outs)
```

```python
import functools

import jax
import jax.numpy as jnp
from jax import lax
from jax.experimental import pallas as pl
from jax.experimental.pallas import tpu as pltpu

F32 = jnp.float32
BF16 = jnp.bfloat16
I32 = jnp.int32

D_MODEL = 1024
GLA_HEADS = 4
GLA_HEAD_K = 64
GLA_HEAD_V = 128
GLA_KEY = GLA_HEADS * GLA_HEAD_K
GLA_VAL = GLA_HEADS * GLA_HEAD_V
GLA_LOWRANK = 16
GLA_TAU = 16.0
GLA_CHUNK = 64
SB_HEADS = 8
SB_HEAD_DIM = 64
SB_WIDTH = SB_HEADS * SB_HEAD_DIM
N_EXPERTS = 32
TOP_K = 4
EXPERT_FF = D_MODEL
SWIGLU_LIMIT = 7.0
SWIGLU_ALPHA = 1.702
RMS_EPS = 1e-6
N_ADA = 6

LOG2E = 1.4426950408889634
LANES = 128
VMEM_LIMIT = 56 * 1024 * 1024
MOE_ROWS = 256
DECODE_PAGES = 8


def _params(n_axes, vmem=VMEM_LIMIT):
    return pltpu.CompilerParams(dimension_semantics=("arbitrary",) * n_axes, vmem_limit_bytes=vmem)


def _sigmoid(x):
    return 1.0 / (1.0 + jnp.exp(-x))


def _softplus(z):
    return jnp.maximum(z, 0.0) + jnp.log(1.0 + jnp.exp(-jnp.abs(z)))


def _softplus2(z2):
    return jnp.maximum(z2, 0.0) + jnp.log2(1.0 + jnp.exp2(-jnp.abs(z2)))


def _rms(x, g):
    ms = jnp.mean(x * x, axis=-1, keepdims=True)
    return x * lax.rsqrt(ms + RMS_EPS) * g


def _dot(a, b):
    return jnp.dot(a, b, preferred_element_type=F32)


def _dot_nt(a, b):
    return lax.dot_general(a, b, (((1,), (1,)), ((), ())), preferred_element_type=F32)


def _dot_tn(a, b):
    return lax.dot_general(a, b, (((0,), (0,)), ((), ())), preferred_element_type=F32)


def _ada_kernel(c_ref, w_ref, b_ref, o_ref):
    c = c_ref[...]
    s = c * _sigmoid(c)
    o_ref[...] = jnp.dot(s, w_ref[...], preferred_element_type=F32,
                         precision=lax.Precision.HIGHEST) + b_ref[...]


def _ada_call(c_all, w_ada, b_ada):
    rows = c_all.shape[0]
    n_out = w_ada.shape[1]
    tn = D_MODEL
    return pl.pallas_call(
        _ada_kernel,
        grid=(n_out // tn,),
        in_specs=[pl.BlockSpec((rows, D_MODEL), lambda j: (0, 0)),
                  pl.BlockSpec((D_MODEL, tn), lambda j: (0, j)),
                  pl.BlockSpec((1, tn), lambda j: (0, j))],
        out_specs=pl.BlockSpec((rows, tn), lambda j: (0, j)),
        out_shape=jax.ShapeDtypeStruct((rows, n_out), F32),
        compiler_params=_params(1),
        name="ada",
    )(c_all, w_ada, b_ada)


def _inproj_kernel(x_ref, sh_ref, sc_ref, g_ref, wg_ref, wlr_ref, wal_ref, bal_ref, wsb_ref, wgt_ref,
                   qa_ref, ka_ref, la_ref, va_ref, ra_ref, qb_ref, kb_ref, vb_ref, kf_ref, vf_ref,
                   sga_ref, sgb_ref, *, kv_transposed):
    x = x_ref[...]
    u = _rms(x, g_ref[...]) * (1.0 + sc_ref[0]) + sh_ref[0]
    ub = u.astype(BF16)

    pg = _dot(ub, wg_ref[...])
    qa_ref[...] = pg[:, :GLA_KEY] * (GLA_HEAD_K ** -0.5)
    ka_ref[...] = pg[:, GLA_KEY:2 * GLA_KEY]
    va_ref[...] = pg[:, 2 * GLA_KEY:2 * GLA_KEY + GLA_VAL].astype(BF16)
    r = pg[:, 2 * GLA_KEY + GLA_VAL:]
    ra_ref[...] = (r * _sigmoid(r)).astype(BF16)

    lr = _dot(ub, wlr_ref[...])
    al = _dot(lr.astype(BF16), wal_ref[...]) + bal_ref[...]
    la_ref[...] = -_softplus(-al) * (1.0 / GLA_TAU)

    ps = _dot(ub, wsb_ref[...])
    qb_ref[...] = (ps[:, :SB_WIDTH] * (LOG2E * SB_HEAD_DIM ** -0.5)).astype(BF16)
    k = ps[:, SB_WIDTH:2 * SB_WIDTH]
    v = ps[:, 2 * SB_WIDTH:]
    if kv_transposed:
        kf_ref[0] = k.T.reshape(SB_HEADS, SB_HEAD_DIM, k.shape[0])
        vf_ref[0] = v.T.reshape(SB_HEADS, SB_HEAD_DIM, v.shape[0])
    else:
        kf_ref[...] = k
        vf_ref[...] = v
    kb_ref[...] = k.astype(BF16)
    vb_ref[...] = v.astype(BF16)

    pt = _dot(ub, wgt_ref[...])
    sga_ref[...] = _sigmoid(pt[:, :D_MODEL]).astype(BF16)
    sgb_ref[...] = _sigmoid(pt[:, D_MODEL:]).astype(BF16)


def _mod_spec(mod, tiles_per_mod):
    return pl.BlockSpec((1,) + mod.shape[1:], lambda i: (i // tiles_per_mod, 0, 0))


def _const_spec(a):
    nd = a.ndim
    return pl.BlockSpec(a.shape, lambda *_: (0,) * nd)


def _inproj_call(x2d, shift, scale, gain, w, tm, tiles_per_mod, kv_batch):
    n = x2d.shape[0]
    widths = [(GLA_KEY, F32), (GLA_KEY, F32), (GLA_KEY, F32), (GLA_VAL, BF16), (GLA_VAL, BF16),
              (SB_WIDTH, BF16), (SB_WIDTH, BF16), (SB_WIDTH, BF16), (SB_WIDTH, F32), (SB_WIDTH, F32),
              (D_MODEL, BF16), (D_MODEL, BF16)]
    row = lambda i: (i, 0)
    out_specs = [pl.BlockSpec((tm, wd), row) for wd, _ in widths]
    out_shape = [jax.ShapeDtypeStruct((n, wd), dt) for wd, dt in widths]
    if kv_batch is not None:
        tiles = n // kv_batch // tm
        for idx in (8, 9):
            out_specs[idx] = pl.BlockSpec((1, SB_HEADS, SB_HEAD_DIM, tm), lambda i: (i // tiles, 0, 0, i % tiles))
            out_shape[idx] = jax.ShapeDtypeStruct((kv_batch, SB_HEADS, SB_HEAD_DIM, n // kv_batch), F32)
    consts = [gain, w["gla"], w["lr"], w["alpha"], w["b_alpha"], w["sb"], w["gates"]]
    return pl.pallas_call(
        functools.partial(_inproj_kernel, kv_transposed=kv_batch is not None),
        grid=(n // tm,),
        in_specs=[pl.BlockSpec((tm, D_MODEL), row), _mod_spec(shift, tiles_per_mod),
                  _mod_spec(scale, tiles_per_mod)] + [_const_spec(a) for a in consts],
        out_specs=out_specs,
        out_shape=out_shape,
        compiler_params=_params(1),
        name="inproj",
    )(x2d, shift, scale, *consts)


def _gla_kernel(q_ref, k_ref, la_ref, v_ref, r_ref, s0_ref, g_ref, o_ref, sn_ref, st_sc, *, n_chunks):
    j = pl.program_id(1)
    C = GLA_CHUNK

    @pl.when(j == 0)
    def _():
        for p in range(2):
            s_pair = jnp.concatenate([s0_ref[0, 2 * p], s0_ref[0, 2 * p + 1]], axis=0)
            st_sc[p] = s_pair.T

    row = lax.broadcasted_iota(I32, (C, C), 0)
    col = lax.broadcasted_iota(I32, (C, C), 1)
    causal = col <= row
    l_incl = jnp.where(causal, 1.0, 0.0).astype(BF16)
    lane = lax.broadcasted_iota(I32, (C, LANES), 1)
    lane_sq = lax.broadcasted_iota(I32, (LANES, LANES), 1)

    def chunk(c, carry):
        sl = pl.ds(pl.multiple_of(c * C, C), C)
        a = la_ref[sl, :]
        a_hi = a.astype(BF16)
        a_lo = (a - a_hi.astype(F32)).astype(BF16)
        b = _dot(l_incl, a_hi) + _dot(l_incl, a_lo)
        b_last = b[C - 1:C, :]
        q_t = q_ref[sl, :] * jnp.exp(b)
        k = k_ref[sl, :]
        k_t = (k * jnp.exp(-b)).astype(BF16)
        k_u = (k * jnp.exp(b_last - b)).astype(BF16)
        dec = jnp.exp(b_last)
        for p in range(2):
            ps = slice(LANES * p, LANES * (p + 1))
            qp = q_t[:, ps]
            kp = k_t[:, ps]
            kup = k_u[:, ps]
            st = st_sc[p]
            stb = st.astype(BF16)
            upd = []
            for hh in range(2):
                h = 2 * p + hh
                hs = slice(GLA_HEAD_V * h, GLA_HEAD_V * (h + 1))
                in_head = (lane >= GLA_HEAD_K * hh) & (lane < GLA_HEAD_K * (hh + 1))
                qm = jnp.where(in_head, qp, 0.0).astype(BF16)
                s = jnp.where(causal, _dot_nt(qm, kp), 0.0)
                vh = v_ref[sl, hs]
                o = _dot_nt(qm, stb) + _dot(s.astype(BF16), vh)
                on = _rms(o, g_ref[...])
                o_ref[sl, hs] = (on * r_ref[sl, hs].astype(F32)).astype(BF16)
                upd.append(_dot_tn(vh, kup))
            st_sc[p] = st * dec[:, ps] + jnp.where(lane_sq < GLA_HEAD_K, upd[0], upd[1])
        return carry

    lax.fori_loop(0, n_chunks, chunk, 0)

    @pl.when(j == pl.num_programs(1) - 1)
    def _():
        for p in range(2):
            s_pair = st_sc[p].T
            sn_ref[0, 2 * p] = s_pair[:GLA_HEAD_K]
            sn_ref[0, 2 * p + 1] = s_pair[GLA_HEAD_K:]


def _gla_call(qa, ka, la, va, ra, s0, g_head, batch, seq, tt):
    n = qa.shape[0]
    nt = seq // tt
    rowmap = lambda b, j: (b * nt + j, 0)
    smap = lambda b, j: (b, 0, 0, 0)
    sblock = (1, GLA_HEADS, GLA_HEAD_K, GLA_HEAD_V)
    return pl.pallas_call(
        functools.partial(_gla_kernel, n_chunks=tt // GLA_CHUNK),
        grid=(batch, nt),
        in_specs=[pl.BlockSpec((tt, GLA_KEY), rowmap), pl.BlockSpec((tt, GLA_KEY), rowmap),
                  pl.BlockSpec((tt, GLA_KEY), rowmap), pl.BlockSpec((tt, GLA_VAL), rowmap),
                  pl.BlockSpec((tt, GLA_VAL), rowmap), pl.BlockSpec(sblock, smap),
                  pl.BlockSpec((1, GLA_HEAD_V), lambda b, j: (0, 0))],
        out_specs=[pl.BlockSpec((tt, GLA_VAL), rowmap), pl.BlockSpec(sblock, smap)],
        out_shape=[jax.ShapeDtypeStruct((n, GLA_VAL), BF16),
                   jax.ShapeDtypeStruct((batch,) + sblock[1:], F32)],
        scratch_shapes=[pltpu.VMEM((2, LANES, LANES), F32)],
        compiler_params=_params(2),
        name="gla",
    )(qa, ka, la, va, ra, s0, g_head)


def _sb_prompt_kernel(bias_ref, q_ref, k_ref, v_ref, o_ref, *, tq):
    pair = pl.program_id(1)
    qi = pl.program_id(2)
    q2 = q_ref[...]
    lane = lax.broadcasted_iota(I32, (tq, LANES), 1)
    row = lax.broadcasted_iota(I32, (tq, tq), 0)
    col = lax.broadcasted_iota(I32, (tq, tq), 1)
    strict = row < col
    later = jnp.where(col > row, 1.0, 0.0).astype(BF16)
    qms = []
    for p in range(2):
        in_head = (lane >= SB_HEAD_DIM * p) & (lane < SB_HEAD_DIM * (p + 1))
        qms.append(jnp.where(in_head, q2, jnp.zeros_like(q2)))
    biases = [bias_ref[2 * pair], bias_ref[2 * pair + 1]]

    def key_blocks(js, masked, carry):
        kv = []
        for j in js:
            ks = pl.ds(pl.multiple_of(j * tq, tq), tq)
            kv.append((k_ref[ks, :], v_ref[ks, :]))
        zs, sps, cums = [], [], []
        for kblk, _ in kv:
            for p in range(2):
                z = _dot_nt(kblk, qms[p]) + biases[p]
                sp = _softplus2(z)
                if masked:
                    sp = jnp.where(strict, sp, 0.0)
                zs.append(z)
                sps.append(sp)
        for sp in sps:
            cums.append(_dot(later, sp.astype(BF16)))
        accs = [carry[0], carry[2]]
        runs = [carry[1], carry[3]]
        for b, (_, vblk) in enumerate(kv):
            for p in range(2):
                i = 2 * b + p
                cum = cums[i] + runs[p]
                a = jnp.exp2(zs[i] - sps[i] - cum)
                if masked:
                    a = jnp.where(strict, a, 0.0)
                accs[p] = accs[p] + _dot_tn(vblk, a.astype(BF16))
                runs[p] = cum[0:1, :] + sps[i][0:1, :]
        return accs[0], runs[0], accs[1], runs[1]

    zacc = jnp.zeros((LANES, tq), F32)
    zrun = jnp.zeros((1, tq), F32)
    carry = key_blocks([qi], True, (zacc, zrun, zacc, zrun))
    odd = qi % 2
    carry = lax.fori_loop(0, odd, lambda _, c: key_blocks([qi - 1], False, c), carry)

    def two_blocks(jj, c):
        j_hi = qi - odd - 1 - 2 * jj
        return key_blocks([j_hi, j_hi - 1], False, c)

    carry = lax.fori_loop(0, qi // 2, two_blocks, carry)
    rows = lax.broadcasted_iota(I32, (LANES, tq), 0)
    o_t = jnp.where(rows < SB_HEAD_DIM, carry[0], carry[2])
    o_ref[...] = o_t.T.astype(BF16)


def _sb_prompt_call(qb, kb, vb, bias, batch, seq, tq):
    n = qb.shape[0]
    nq = seq // tq
    return pl.pallas_call(
        functools.partial(_sb_prompt_kernel, tq=tq),
        grid=(batch, SB_HEADS // 2, nq),
        in_specs=[pl.BlockSpec(memory_space=pltpu.SMEM),
                  pl.BlockSpec((tq, LANES), lambda b, p, i: (b * nq + i, p)),
                  pl.BlockSpec((seq, LANES), lambda b, p, i: (b, p)),
                  pl.BlockSpec((seq, LANES), lambda b, p, i: (b, p))],
        out_specs=pl.BlockSpec((tq, LANES), lambda b, p, i: (b * nq + i, p)),
        out_shape=jax.ShapeDtypeStruct((n, SB_WIDTH), BF16),
        compiler_params=_params(3),
        name="sb_prompt",
    )(bias, qb, kb, vb)


def _sb_decode_kernel(pt_ref, qc_ref, bias_ref, tri_ref, *refs, n_pages):
    k_refs = refs[:n_pages]
    v_refs = refs[n_pages:2 * n_pages]
    o_ref, acc_sc, run_sc, z_sc = refs[2 * n_pages:]
    g = pl.program_id(1)
    heads, d, page = k_refs[0].shape

    @pl.when(g == 0)
    def _():
        acc_sc[...] = jnp.zeros_like(acc_sc)
        run_sc[...] = jnp.zeros_like(run_sc)

    for i in range(n_pages):
        for h in range(heads):
            prod = k_refs[i][h] * qc_ref[0, h]
            r = i * heads + h
            z_sc[r:r + 1, :] = jnp.sum(prod, axis=0, keepdims=True)
    z = z_sc[...] + bias_ref[...]
    sp = _softplus2(z)
    cum = _dot(sp.astype(BF16), tri_ref[...])
    tot = jnp.sum(sp, axis=1, keepdims=True)
    lw = z - sp - cum
    run = run_sc[:, 0:1]
    weights = []
    for i in range(n_pages):
        rows = slice(i * heads, (i + 1) * heads)
        weights.append(jnp.exp2(lw[rows] - run))
        run = run + tot[rows]
    run_sc[...] = jnp.broadcast_to(run, run_sc.shape)
    for h in range(heads):
        acc = acc_sc[h]
        for i in range(n_pages):
            acc = acc + v_refs[i][h] * weights[i][h:h + 1, :]
        acc_sc[h] = acc

    @pl.when(g == pl.num_programs(1) - 1)
    def _():
        ones = jnp.ones((8, page), F32)
        for h in range(heads):
            tot = lax.dot_general(ones, acc_sc[h], (((1,), (1,)), ((), ())), preferred_element_type=F32,
                                  precision=lax.Precision.HIGHEST)
            o_ref[0, h:h + 1, :] = tot[0:1, :]


def _sb_decode_call(qc, bias, tri, cache_k_t, cache_v_t, page_table, layer):
    nb, heads, d, page = qc.shape
    n_seq_pages = page_table.shape[1]
    gp = DECODE_PAGES
    while n_seq_pages % gp:
        gp //= 2
    ng = n_seq_pages // gp
    bias_col = jnp.tile(bias.reshape(heads, 1), (gp, 1))

    def page_spec(i):
        def imap(b, g, pt):
            return (layer, pt[b, n_seq_pages - 1 - (g * gp + i)], 0, 0, 0)
        return pl.BlockSpec((None, None, heads, d, page), imap)

    grid_spec = pltpu.PrefetchScalarGridSpec(
        num_scalar_prefetch=1,
        grid=(nb, ng),
        in_specs=[pl.BlockSpec((1, heads, d, page), lambda b, g, pt: (b, 0, 0, 0)),
                  pl.BlockSpec((gp * heads, 1), lambda b, g, pt: (0, 0)),
                  pl.BlockSpec(tri.shape, lambda b, g, pt: (0, 0))]
                 + [page_spec(i) for i in range(gp)] * 2,
        out_specs=pl.BlockSpec((1, heads, d), lambda b, g, pt: (b, 0, 0)),
        scratch_shapes=[pltpu.VMEM((heads, d, page), F32), pltpu.VMEM((heads, LANES), F32),
                        pltpu.VMEM((gp * heads, page), F32)],
    )
    return pl.pallas_call(
        functools.partial(_sb_decode_kernel, n_pages=gp),
        grid_spec=grid_spec,
        out_shape=jax.ShapeDtypeStruct((nb, heads, d), F32),
        compiler_params=_params(2),
        name="sb_decode",
    )(page_table, qc, bias_col, tri, *([cache_k_t] * gp), *([cache_v_t] * gp))


def _merge_kernel(oa_ref, ob_ref, sga_ref, sgb_ref, x_ref, g1_ref, sh2_ref, sc2_ref,
                  wa_ref, wb_ref, wo_ref, gpm_ref, gpf_ref, wr_ref, br_ref,
                  x1_ref, h_ref, ti_ref, tp_ref):
    br_a = _dot(oa_ref[...], wa_ref[...])
    br_b = _dot(ob_ref[...], wb_ref[...])
    m = sga_ref[...].astype(F32) * br_a + sgb_ref[...].astype(F32) * br_b
    mixed = _dot(m.astype(BF16), wo_ref[...])
    x1 = x_ref[...] + g1_ref[0] * _rms(mixed, gpm_ref[...])
    x1_ref[...] = x1
    h = _rms(x1, gpf_ref[...]) * (1.0 + sc2_ref[0]) + sh2_ref[0]
    h_ref[...] = h

    logits = lax.dot_general(wr_ref[...], h, (((1,), (1,)), ((), ())), preferred_element_type=F32,
                             precision=lax.Precision.HIGHEST) + br_ref[...]
    eidx = lax.broadcasted_iota(I32, logits.shape, 0).astype(F32)
    vals = []
    for kk in range(TOP_K):
        mx = jnp.max(logits, axis=0, keepdims=True)
        sel = jnp.min(jnp.where(logits == mx, eidx, float(N_EXPERTS)), axis=0, keepdims=True)
        ti_ref[kk:kk + 1, :] = sel.astype(I32)
        vals.append(mx)
        logits = jnp.where(eidx == sel, -jnp.inf, logits)
    es = [jnp.exp(v - vals[0]) for v in vals]
    tot = es[0] + es[1] + es[2] + es[3]
    for kk in range(TOP_K):
        tp_ref[kk:kk + 1, :] = es[kk] / tot


def _merge_call(oa, ob, sga, sgb, x2d, gate1, shift2, scale2, w, tm, tiles_per_mod):
    n = x2d.shape[0]
    row = lambda i: (i, 0)
    colb = lambda i: (0, i)
    consts = [w["gla_o"], w["sb_o"], w["out"], w["g_post_mix"], w["g_pre_ffn"], w["router_t"], w["b_router"]]
    return pl.pallas_call(
        _merge_kernel,
        grid=(n // tm,),
        in_specs=[pl.BlockSpec((tm, GLA_VAL), row), pl.BlockSpec((tm, SB_WIDTH), row),
                  pl.BlockSpec((tm, D_MODEL), row), pl.BlockSpec((tm, D_MODEL), row),
                  pl.BlockSpec((tm, D_MODEL), row),
                  _mod_spec(gate1, tiles_per_mod), _mod_spec(shift2, tiles_per_mod),
                  _mod_spec(scale2, tiles_per_mod)] + [_const_spec(a) for a in consts],
        out_specs=[pl.BlockSpec((tm, D_MODEL), row), pl.BlockSpec((tm, D_MODEL), row),
                   pl.BlockSpec((TOP_K, tm), colb), pl.BlockSpec((TOP_K, tm), colb)],
        out_shape=[jax.ShapeDtypeStruct((n, D_MODEL), F32), jax.ShapeDtypeStruct((n, D_MODEL), F32),
                   jax.ShapeDtypeStruct((TOP_K, n), I32), jax.ShapeDtypeStruct((TOP_K, n), F32)],
        compiler_params=_params(1),
        name="merge",
    )(oa, ob, sga, sgb, x2d, gate1, shift2, scale2, *consts)


def _row_copy(src, src_row, dst, dst_row, sem):
    return pltpu.make_async_copy(src.at[pl.ds(src_row, 1), :], dst.at[pl.ds(dst_row, 1), :], sem)


def _moe_kernel(be_ref, tok_next_ref, tok_first_ref, slot_prev_ref, slot_cur_ref, h_ref,
                wgu_ref, bgu_ref, wd_ref, bd_ref, yk_ref,
                xbuf, ybuf, wgu_sc, wd_sc, tok_smem, slot_smem, idx_sem, gsem, ssem):
    b = pl.program_id(0)
    nb = pl.num_programs(0)
    cur = b % 2
    nxt = 1 - cur
    bm = xbuf.shape[1]

    def gather_wait(s):
        pltpu.make_async_copy(h_ref.at[pl.ds(0, bm), :], xbuf.at[s], gsem.at[s]).wait()

    def scatter_wait(s):
        pltpu.make_async_copy(ybuf.at[s], yk_ref.at[pl.ds(0, bm), :], ssem.at[s]).wait()

    def load_indices(src_ref, dst_smem):
        cp = pltpu.make_async_copy(src_ref.at[0], dst_smem, idx_sem)
        cp.start()
        cp.wait()

    @pl.when(b == 0)
    def _():
        ybuf[1] = jnp.zeros(ybuf.shape[1:], F32)
        load_indices(tok_first_ref, tok_smem)

        def first(r, c):
            _row_copy(h_ref, tok_smem[0, r], xbuf.at[0], r, gsem.at[0]).start()
            return c

        lax.fori_loop(0, bm, first, 0)

    @pl.when(b >= 1)
    def _():
        scatter_wait(cur)

    load_indices(tok_next_ref, tok_smem)
    load_indices(slot_prev_ref, slot_smem)

    first_of_expert = (b == 0) | (be_ref[b] != be_ref[jnp.maximum(b - 1, 0)])

    @pl.when(first_of_expert)
    def _():
        wgu_sc[...] = wgu_ref[0].astype(BF16)
        wd_sc[...] = wd_ref[0].astype(BF16)

    def block(s):
        gather_wait(s)
        x = xbuf[s].astype(BF16)
        for r in range(bm):
            _row_copy(h_ref, tok_smem[0, r], xbuf.at[1 - s], r, gsem.at[1 - s]).start()
            _row_copy(ybuf.at[1 - s], r, yk_ref, slot_smem[0, r], ssem.at[1 - s]).start()
        gu = _dot(x, wgu_sc[...]) + bgu_ref[0]
        g = jnp.minimum(gu[:, :EXPERT_FF], SWIGLU_LIMIT)
        u = jnp.clip(gu[:, EXPERT_FF:], -SWIGLU_LIMIT, SWIGLU_LIMIT)
        act = (u + 1.0) * (g * _sigmoid(SWIGLU_ALPHA * g))
        ybuf[s] = _dot(act.astype(BF16), wd_sc[...]) + bd_ref[0]

    for s in range(2):
        pl.when(cur == s)(functools.partial(block, s))

    @pl.when(b == nb - 1)
    def _():
        load_indices(slot_cur_ref, slot_smem)

        def last(r, c):
            _row_copy(ybuf.at[cur], r, yk_ref, slot_smem[0, r], ssem.at[cur]).start()
            return c

        lax.fori_loop(0, bm, last, 0)
        scatter_wait(nxt)
        scatter_wait(cur)
        gather_wait(nxt)


def _moe_call(route, h, w):
    row_tok, row_slot, row_slot_prev, block_e, n_out_rows = route
    nb, _, bm = row_tok.shape
    emap = lambda i, be: (be[i], 0, 0)
    idx_block = (1, 1, bm)
    grid_spec = pltpu.PrefetchScalarGridSpec(
        num_scalar_prefetch=1,
        grid=(nb,),
        in_specs=[pl.BlockSpec(idx_block, lambda i, be: (jnp.minimum(i + 1, nb - 1), 0, 0)),
                  pl.BlockSpec(idx_block, lambda i, be: (0, 0, 0)),
                  pl.BlockSpec(idx_block, lambda i, be: (i, 0, 0)),
                  pl.BlockSpec(idx_block, lambda i, be: (i, 0, 0)),
                  pl.BlockSpec(memory_space=pl.ANY),
                  pl.BlockSpec((1, D_MODEL, 2 * EXPERT_FF), emap),
                  pl.BlockSpec((1, 1, 2 * EXPERT_FF), emap),
                  pl.BlockSpec((1, EXPERT_FF, D_MODEL), emap),
                  pl.BlockSpec((1, 1, D_MODEL), emap)],
        out_specs=pl.BlockSpec(memory_space=pl.ANY),
        scratch_shapes=[pltpu.VMEM((2, bm, D_MODEL), F32), pltpu.VMEM((2, bm, D_MODEL), F32),
                        pltpu.VMEM((D_MODEL, 2 * EXPERT_FF), BF16), pltpu.VMEM((EXPERT_FF, D_MODEL), BF16),
                        pltpu.SMEM((1, bm), I32), pltpu.SMEM((1, bm), I32),
                        pltpu.SemaphoreType.DMA, pltpu.SemaphoreType.DMA((2,)), pltpu.SemaphoreType.DMA((2,))],
    )
    return pl.pallas_call(
        _moe_kernel,
        grid_spec=grid_spec,
        out_shape=jax.ShapeDtypeStruct((n_out_rows, D_MODEL), F32),
        compiler_params=_params(1),
        name="moe",
    )(block_e, row_tok, row_tok, row_slot_prev, row_slot, h,
      w["gate_up"], w["b_gate_up"], w["down"], w["b_down"])


def _final_kernel(p_ref, x1_ref, g2_ref, gpf_ref, *refs):
    y_refs, out_ref = refs[:TOP_K], refs[TOP_K]
    p = p_ref[...]
    f = p[:, 0:1] * y_refs[0][...]
    for kk in range(1, TOP_K):
        f = f + p[:, kk:kk + 1] * y_refs[kk][...]
    out_ref[...] = x1_ref[...] + g2_ref[0] * _rms(f, gpf_ref[...])


def _final_call(probs_t, x1, gate2, g_post_ffn, yk, tm, tiles_per_mod):
    n = x1.shape[0]
    nt = n // tm
    y_specs = [pl.BlockSpec((tm, D_MODEL), lambda i, kk=kk: (kk * nt + i, 0)) for kk in range(TOP_K)]
    return pl.pallas_call(
        _final_kernel,
        grid=(nt,),
        in_specs=[pl.BlockSpec((tm, TOP_K), lambda i: (i, 0)),
                  pl.BlockSpec((tm, D_MODEL), lambda i: (i, 0)),
                  _mod_spec(gate2, tiles_per_mod), _const_spec(g_post_ffn)] + y_specs,
        out_specs=pl.BlockSpec((tm, D_MODEL), lambda i: (i, 0)),
        out_shape=jax.ShapeDtypeStruct((n, D_MODEL), F32),
        compiler_params=_params(1),
        name="final",
    )(probs_t, x1, gate2, g_post_ffn, *([yk] * TOP_K))


def _route(top_i, bm):
    k, n = top_i.shape
    nk = k * n
    ef = top_i.reshape(nk)
    order = jnp.argsort(ef).astype(I32)
    experts = jnp.arange(N_EXPERTS, dtype=I32)
    counts = jnp.sum((ef[:, None] == experts[None, :]).astype(I32), axis=0)
    start = jnp.cumsum(counts) - counts
    padded = (counts + bm - 1) // bm * bm
    pend = jnp.cumsum(padded)
    pstart = pend - padded
    n_blocks = -(-nk // bm) + N_EXPERTS
    blocks = jnp.arange(n_blocks, dtype=I32)
    first_row = blocks * bm
    block_e = jnp.sum((pend[None, :] <= first_row[:, None]).astype(I32), axis=1)
    block_e = jnp.minimum(block_e, N_EXPERTS - 1).astype(I32)
    sel = (block_e[:, None] == experts[None, :]).astype(I32)
    in_expert = first_row - jnp.sum(sel * pstart[None, :], axis=1)
    n_valid = jnp.clip(jnp.sum(sel * counts[None, :], axis=1) - in_expert, 0, bm)
    src = jnp.clip(jnp.sum(sel * start[None, :], axis=1) + in_expert, 0, nk)
    order_pad = jnp.concatenate([order, jnp.zeros((bm,), I32)])
    row_f = jax.vmap(lambda o: lax.dynamic_slice(order_pad, (o,), (bm,)))(src)
    lanes = jnp.arange(bm, dtype=I32)[None, :]
    valid = lanes < n_valid[:, None]
    row_tok = jnp.where(valid, row_f % n, 0)
    spare = nk + (blocks[:, None] % 2) * bm + lanes
    row_slot = jnp.where(valid, row_f, spare)
    placeholder = nk + 2 * bm + lanes
    row_slot_prev = jnp.concatenate([placeholder, row_slot[:-1]], axis=0)
    as3d = lambda a: a.astype(I32).reshape(n_blocks, 1, bm)
    return as3d(row_tok), as3d(row_slot), as3d(row_slot_prev), block_e, nk + 3 * bm


def _pick_tile(n, pref):
    t = pref
    while n % t:
        t //= 2
    return t


def _layer(x, ada, s0, w, sb_fn):
    batch, seq, _ = x.shape
    n = batch * seq
    x2d = x.reshape(n, D_MODEL)
    if seq == 1:
        tm = _pick_tile(n, 128)
        mods = [ada[:, i].reshape(n // tm, tm, D_MODEL) for i in range(N_ADA)]
        tiles_per_mod = 1
    else:
        tm = _pick_tile(seq, 256)
        mods = [ada[:, i].reshape(batch, 1, D_MODEL) for i in range(N_ADA)]
        tiles_per_mod = seq // tm
    shift1, scale1, gate1, shift2, scale2, gate2 = mods

    kv_batch = batch if seq % LANES == 0 else None
    qa, ka, la, va, ra, qb, kb, vb, kf, vf, sga, sgb = _inproj_call(
        x2d, shift1, scale1, w["g_pre_mix"], w, tm, tiles_per_mod, kv_batch)

    seq_p = -(-seq // GLA_CHUNK) * GLA_CHUNK
    if seq_p != seq:
        def pad_t(a):
            a = a.reshape(batch, seq, a.shape[-1])
            return jnp.pad(a, ((0, 0), (0, seq_p - seq), (0, 0))).reshape(batch * seq_p, a.shape[-1])
        gq, gk, gl, gv, gr = [pad_t(a) for a in (qa, ka, la, va, ra)]
    else:
        gq, gk, gl, gv, gr = qa, ka, la, va, ra
    tt = _pick_tile(seq_p, 512)
    oa, s_new = _gla_call(gq, gk, gl, gv, gr, s0, w["g_gla_head"], batch, seq_p, tt)
    if seq_p != seq:
        oa = oa.reshape(batch, seq_p, GLA_VAL)[:, :seq].reshape(n, GLA_VAL)

    ob = sb_fn(qb, kb, vb)

    x1, h, top_i, top_p = _merge_call(oa, ob, sga, sgb, x2d, gate1, shift2, scale2, w, tm, tiles_per_mod)

    yk = _moe_call(_route(top_i, MOE_ROWS), h, w)
    y = _final_call(top_p.T, x1, gate2, w["g_post_ffn"], yk, tm, tiles_per_mod)

    if kv_batch is not None:
        kf, vf = [jnp.transpose(a, (0, 3, 1, 2)) for a in (kf, vf)]
    kv_shape = (batch, seq, SB_HEADS, SB_HEAD_DIM)
    return y.reshape(batch, seq, D_MODEL), kf.reshape(kv_shape), vf.reshape(kv_shape), s_new


def _layer_weights(l, w_in, w_alpha, b_alpha, g_pre_mix, g_gla_head, w_gla_o, w_sb_o, w_out, g_post_mix,
                   g_pre_ffn, w_router, b_router, w_gate_up, b_gate_up, w_down, b_down, g_post_ffn):
    wi = w_in[l]
    o_lr = 2 * GLA_KEY + 2 * GLA_VAL
    o_sb = o_lr + GLA_LOWRANK
    o_gt = o_sb + 3 * SB_WIDTH
    w_lr = jnp.pad(wi[:, o_lr:o_sb], ((0, 0), (0, LANES - GLA_LOWRANK)))
    w_al = jnp.pad(w_alpha[l], ((0, LANES - GLA_LOWRANK), (0, 0)))
    row = lambda a: a[l].reshape(1, -1)
    return {
        "gla": wi[:, :o_lr].astype(BF16), "lr": w_lr.astype(BF16), "alpha": w_al.astype(BF16),
        "b_alpha": row(b_alpha), "sb": wi[:, o_sb:o_gt].astype(BF16), "gates": wi[:, o_gt:].astype(BF16),
        "g_pre_mix": row(g_pre_mix), "g_gla_head": row(g_gla_head),
        "gla_o": w_gla_o[l].astype(BF16), "sb_o": w_sb_o[l].astype(BF16), "out": w_out[l].astype(BF16),
        "g_post_mix": row(g_post_mix), "g_pre_ffn": row(g_pre_ffn),
        "router_t": w_router[l].T, "b_router": b_router[l].reshape(N_EXPERTS, 1),
        "gate_up": w_gate_up[l], "b_gate_up": b_gate_up[l].reshape(N_EXPERTS, 1, -1),
        "down": w_down[l], "b_down": b_down[l].reshape(N_EXPERTS, 1, -1),
        "g_post_ffn": row(g_post_ffn),
    }


def kernel(x_prompt, x_sample, cache_sb_k, cache_sb_v, state_gla, page_table, c_prompt, c_sample, w_ada, b_ada, g_pre_mix, w_in, w_alpha, b_alpha, g_gla_head, w_gla_o, b_sb_logit, w_sb_o, w_out, g_post_mix, g_pre_ffn, w_router, b_router, w_gate_up, b_gate_up, w_down, b_down, g_post_ffn):
    depth = w_in.shape[0]
    bp, sp, _ = x_prompt.shape
    bs, ss, _ = x_sample.shape
    assert ss == 1, "the sample group decodes one token per sequence"
    page = cache_sb_k.shape[2]
    ridx = jnp.arange(page, dtype=I32)
    tri = (ridx[:, None] > ridx[None, :]).astype(BF16)
    cache_k_t = jnp.transpose(cache_sb_k, (0, 1, 3, 4, 2))
    cache_v_t = jnp.transpose(cache_sb_v, (0, 1, 3, 4, 2))

    yp, ys = x_prompt, x_sample
    outs = [[] for _ in range(6)]
    for l in range(depth):
        w = _layer_weights(l, w_in, w_alpha, b_alpha, g_pre_mix, g_gla_head, w_gla_o, w_sb_o, w_out, g_post_mix,
                           g_pre_ffn, w_router, b_router, w_gate_up, b_gate_up, w_down, b_down, g_post_ffn)
        c_all = jnp.concatenate([c_prompt, c_sample], axis=0)
        ada = _ada_call(c_all, w_ada[l], b_ada[l].reshape(1, -1))
        ada_p = ada[:bp].reshape(bp, N_ADA, D_MODEL)
        ada_s = ada[bp:].reshape(bs, N_ADA, D_MODEL)
        bias = b_sb_logit[l] * LOG2E

        tq = _pick_tile(sp, 256)
        sb_prompt = lambda qb, kb, vb: _sb_prompt_call(qb, kb, vb, bias, bp, sp, tq)
        s0_p = jnp.zeros((bp, GLA_HEADS, GLA_HEAD_K, GLA_HEAD_V), F32)
        yp, kp, vp, stp = _layer(yp, ada_p, s0_p, w, sb_prompt)

        def sb_sample(qb, kb, vb):
            q3 = qb.reshape(bs, SB_HEADS, SB_HEAD_DIM).astype(F32)
            qc = jnp.broadcast_to(q3[..., None], q3.shape + (page,))
            o = _sb_decode_call(qc, bias, tri, cache_k_t, cache_v_t, page_table, l)
            return o.reshape(bs, SB_WIDTH).astype(BF16)

        ys, kn, vn, stn = _layer(ys, ada_s, state_gla[l], w, sb_sample)
        for lst, val in zip(outs, (kp, vp, stp, kn, vn, stn)):
            lst.append(val)
    return (yp, ys) + tuple(jnp.stack(o) for o in outs)
```

```python
import functools

import jax
import jax.numpy as jnp
from jax import lax
from jax.experimental import pallas as pl
from jax.experimental.pallas import tpu as pltpu

F32 = jnp.float32
BF16 = jnp.bfloat16
I32 = jnp.int32

D_MODEL = 1024
GLA_HEADS = 4
GLA_HEAD_K = 64
GLA_HEAD_V = 128
GLA_KEY = GLA_HEADS * GLA_HEAD_K
GLA_VAL = GLA_HEADS * GLA_HEAD_V
GLA_LOWRANK = 16
GLA_TAU = 16.0
GLA_CHUNK = 64
SB_HEADS = 8
SB_HEAD_DIM = 64
SB_WIDTH = SB_HEADS * SB_HEAD_DIM
N_EXPERTS = 32
TOP_K = 4
EXPERT_FF = D_MODEL
SWIGLU_LIMIT = 7.0
SWIGLU_ALPHA = 1.702
RMS_EPS = 1e-6
N_ADA = 6

LOG2E = 1.4426950408889634
LANES = 128
VMEM_LIMIT = 56 * 1024 * 1024
MOE_ROWS = 256
DECODE_PAGES = 8
ROW_COPY_UNROLL = 8


def _params(n_axes, vmem=VMEM_LIMIT):
    return pltpu.CompilerParams(dimension_semantics=("arbitrary",) * n_axes, vmem_limit_bytes=vmem)


def _sigmoid(x):
    return 1.0 / (1.0 + jnp.exp(-x))


def _softplus(z):
    return jnp.maximum(z, 0.0) + jnp.log(1.0 + jnp.exp(-jnp.abs(z)))


def _softplus2(z2):
    return jnp.maximum(z2, 0.0) + jnp.log2(1.0 + jnp.exp2(-jnp.abs(z2)))


def _rms(x, g):
    ms = jnp.mean(x * x, axis=-1, keepdims=True)
    return x * lax.rsqrt(ms + RMS_EPS) * g


def _dot(a, b):
    return jnp.dot(a, b, preferred_element_type=F32)


def _dot_nt(a, b):
    return lax.dot_general(a, b, (((1,), (1,)), ((), ())), preferred_element_type=F32)


def _dot_tn(a, b):
    return lax.dot_general(a, b, (((0,), (0,)), ((), ())), preferred_element_type=F32)


def _ada_kernel(c_ref, w_ref, b_ref, o_ref):
    c = c_ref[...]
    s = c * _sigmoid(c)
    o_ref[...] = jnp.dot(s, w_ref[...], preferred_element_type=F32,
                         precision=lax.Precision.HIGHEST) + b_ref[...]


def _ada_call(c_all, w_ada, b_ada):
    rows = c_all.shape[0]
    n_out = w_ada.shape[1]
    tn = D_MODEL
    return pl.pallas_call(
        _ada_kernel,
        grid=(n_out // tn,),
        in_specs=[pl.BlockSpec((rows, D_MODEL), lambda j: (0, 0)),
                  pl.BlockSpec((D_MODEL, tn), lambda j: (0, j)),
                  pl.BlockSpec((1, tn), lambda j: (0, j))],
        out_specs=pl.BlockSpec((rows, tn), lambda j: (0, j)),
        out_shape=jax.ShapeDtypeStruct((rows, n_out), F32),
        compiler_params=_params(1),
        name="ada",
    )(c_all, w_ada, b_ada)


def _inproj_kernel(x_ref, sh_ref, sc_ref, g_ref, wg_ref, wlr_ref, wal_ref, bal_ref, wsb_ref, wgt_ref,
                   qa_ref, ka_ref, la_ref, va_ref, ra_ref, qb_ref, kb_ref, vb_ref, kf_ref, vf_ref,
                   sga_ref, sgb_ref, *, kv_transposed):
    x = x_ref[...]
    u = _rms(x, g_ref[...]) * (1.0 + sc_ref[0]) + sh_ref[0]
    ub = u.astype(BF16)

    pg = _dot(ub, wg_ref[...])
    qa_ref[...] = pg[:, :GLA_KEY] * (GLA_HEAD_K ** -0.5)
    ka_ref[...] = pg[:, GLA_KEY:2 * GLA_KEY]
    va_ref[...] = pg[:, 2 * GLA_KEY:2 * GLA_KEY + GLA_VAL].astype(BF16)
    r = pg[:, 2 * GLA_KEY + GLA_VAL:]
    ra_ref[...] = (r * _sigmoid(r)).astype(BF16)

    lr = _dot(ub, wlr_ref[...])
    al = _dot(lr.astype(BF16), wal_ref[...]) + bal_ref[...]
    la_ref[...] = -_softplus(-al) * (1.0 / GLA_TAU)

    ps = _dot(ub, wsb_ref[...])
    qb_ref[...] = (ps[:, :SB_WIDTH] * (LOG2E * SB_HEAD_DIM ** -0.5)).astype(BF16)
    k = ps[:, SB_WIDTH:2 * SB_WIDTH]
    v = ps[:, 2 * SB_WIDTH:]
    if kv_transposed:
        kf_ref[0] = k.T.reshape(SB_HEADS, SB_HEAD_DIM, k.shape[0])
        vf_ref[0] = v.T.reshape(SB_HEADS, SB_HEAD_DIM, v.shape[0])
    else:
        kf_ref[...] = k
        vf_ref[...] = v
    kb_ref[...] = k.astype(BF16)
    vb_ref[...] = v.astype(BF16)

    pt = _dot(ub, wgt_ref[...])
    sga_ref[...] = _sigmoid(pt[:, :D_MODEL]).astype(BF16)
    sgb_ref[...] = _sigmoid(pt[:, D_MODEL:]).astype(BF16)


def _mod_spec(mod, tiles_per_mod):
    return pl.BlockSpec((1,) + mod.shape[1:], lambda i: (i // tiles_per_mod, 0, 0))


def _const_spec(a):
    nd = a.ndim
    return pl.BlockSpec(a.shape, lambda *_: (0,) * nd)


def _inproj_call(x2d, shift, scale, gain, w, tm, tiles_per_mod, kv_batch):
    n = x2d.shape[0]
    widths = [(GLA_KEY, F32), (GLA_KEY, F32), (GLA_KEY, F32), (GLA_VAL, BF16), (GLA_VAL, BF16),
              (SB_WIDTH, BF16), (SB_WIDTH, BF16), (SB_WIDTH, BF16), (SB_WIDTH, F32), (SB_WIDTH, F32),
              (D_MODEL, BF16), (D_MODEL, BF16)]
    row = lambda i: (i, 0)
    out_specs = [pl.BlockSpec((tm, wd), row) for wd, _ in widths]
    out_shape = [jax.ShapeDtypeStruct((n, wd), dt) for wd, dt in widths]
    if kv_batch is not None:
        tiles = n // kv_batch // tm
        for idx in (8, 9):
            out_specs[idx] = pl.BlockSpec((1, SB_HEADS, SB_HEAD_DIM, tm), lambda i: (i // tiles, 0, 0, i % tiles))
            out_shape[idx] = jax.ShapeDtypeStruct((kv_batch, SB_HEADS, SB_HEAD_DIM, n // kv_batch), F32)
    consts = [gain, w["gla"], w["lr"], w["alpha"], w["b_alpha"], w["sb"], w["gates"]]
    return pl.pallas_call(
        functools.partial(_inproj_kernel, kv_transposed=kv_batch is not None),
        grid=(n // tm,),
        in_specs=[pl.BlockSpec((tm, D_MODEL), row), _mod_spec(shift, tiles_per_mod),
                  _mod_spec(scale, tiles_per_mod)] + [_const_spec(a) for a in consts],
        out_specs=out_specs,
        out_shape=out_shape,
        compiler_params=_params(1),
        name="inproj",
    )(x2d, shift, scale, *consts)


def _gla_kernel(q_ref, k_ref, la_ref, v_ref, r_ref, s0_ref, g_ref, o_ref, sn_ref, st_sc, *, n_chunks):
    j = pl.program_id(1)
    C = GLA_CHUNK

    @pl.when(j == 0)
    def _():
        for p in range(2):
            s_pair = jnp.concatenate([s0_ref[0, 2 * p], s0_ref[0, 2 * p + 1]], axis=0)
            st_sc[p] = s_pair.T

    row = lax.broadcasted_iota(I32, (C, C), 0)
    col = lax.broadcasted_iota(I32, (C, C), 1)
    causal = col <= row
    l_incl = jnp.where(causal, 1.0, 0.0).astype(BF16)
    lane = lax.broadcasted_iota(I32, (C, LANES), 1)
    lane_sq = lax.broadcasted_iota(I32, (LANES, LANES), 1)

    def chunk(c, carry):
        sl = pl.ds(pl.multiple_of(c * C, C), C)
        a = la_ref[sl, :]
        a_hi = a.astype(BF16)
        a_lo = (a - a_hi.astype(F32)).astype(BF16)
        b = _dot(l_incl, a_hi) + _dot(l_incl, a_lo)
        b_last = b[C - 1:C, :]
        q_t = q_ref[sl, :] * jnp.exp(b)
        k = k_ref[sl, :]
        k_t = (k * jnp.exp(-b)).astype(BF16)
        k_u = (k * jnp.exp(b_last - b)).astype(BF16)
        dec = jnp.exp(b_last)
        for p in range(2):
            ps = slice(LANES * p, LANES * (p + 1))
            qp = q_t[:, ps]
            kp = k_t[:, ps]
            kup = k_u[:, ps]
            st = st_sc[p]
            stb = st.astype(BF16)
            upd = []
            for hh in range(2):
                h = 2 * p + hh
                hs = slice(GLA_HEAD_V * h, GLA_HEAD_V * (h + 1))
                in_head = (lane >= GLA_HEAD_K * hh) & (lane < GLA_HEAD_K * (hh + 1))
                qm = jnp.where(in_head, qp, 0.0).astype(BF16)
                s = jnp.where(causal, _dot_nt(qm, kp), 0.0)
                vh = v_ref[sl, hs]
                o = _dot_nt(qm, stb) + _dot(s.astype(BF16), vh)
                on = _rms(o, g_ref[...])
                o_ref[sl, hs] = (on * r_ref[sl, hs].astype(F32)).astype(BF16)
                upd.append(_dot_tn(vh, kup))
            st_sc[p] = st * dec[:, ps] + jnp.where(lane_sq < GLA_HEAD_K, upd[0], upd[1])
        return carry

    lax.fori_loop(0, n_chunks, chunk, 0)

    @pl.when(j == pl.num_programs(1) - 1)
    def _():
        for p in range(2):
            s_pair = st_sc[p].T
            sn_ref[0, 2 * p] = s_pair[:GLA_HEAD_K]
            sn_ref[0, 2 * p + 1] = s_pair[GLA_HEAD_K:]


def _gla_call(qa, ka, la, va, ra, s0, g_head, batch, seq, tt):
    n = qa.shape[0]
    nt = seq // tt
    rowmap = lambda b, j: (b * nt + j, 0)
    smap = lambda b, j: (b, 0, 0, 0)
    sblock = (1, GLA_HEADS, GLA_HEAD_K, GLA_HEAD_V)
    return pl.pallas_call(
        functools.partial(_gla_kernel, n_chunks=tt // GLA_CHUNK),
        grid=(batch, nt),
        in_specs=[pl.BlockSpec((tt, GLA_KEY), rowmap), pl.BlockSpec((tt, GLA_KEY), rowmap),
                  pl.BlockSpec((tt, GLA_KEY), rowmap), pl.BlockSpec((tt, GLA_VAL), rowmap),
                  pl.BlockSpec((tt, GLA_VAL), rowmap), pl.BlockSpec(sblock, smap),
                  pl.BlockSpec((1, GLA_HEAD_V), lambda b, j: (0, 0))],
        out_specs=[pl.BlockSpec((tt, GLA_VAL), rowmap), pl.BlockSpec(sblock, smap)],
        out_shape=[jax.ShapeDtypeStruct((n, GLA_VAL), BF16),
                   jax.ShapeDtypeStruct((batch,) + sblock[1:], F32)],
        scratch_shapes=[pltpu.VMEM((2, LANES, LANES), F32)],
        compiler_params=_params(2),
        name="gla",
    )(qa, ka, la, va, ra, s0, g_head)


def _sb_prompt_kernel(bias_ref, q_ref, k_ref, v_ref, o_ref, *, tq):
    pair = pl.program_id(1)
    qi = pl.program_id(2)
    q2 = q_ref[...]
    lane = lax.broadcasted_iota(I32, (tq, LANES), 1)
    row = lax.broadcasted_iota(I32, (tq, tq), 0)
    col = lax.broadcasted_iota(I32, (tq, tq), 1)
    strict = row < col
    later = jnp.where(col > row, 1.0, 0.0).astype(BF16)
    qms = []
    for p in range(2):
        in_head = (lane >= SB_HEAD_DIM * p) & (lane < SB_HEAD_DIM * (p + 1))
        qms.append(jnp.where(in_head, q2, jnp.zeros_like(q2)))
    biases = [bias_ref[2 * pair], bias_ref[2 * pair + 1]]

    def key_blocks(js, masked, carry):
        kv = []
        for j in js:
            ks = pl.ds(pl.multiple_of(j * tq, tq), tq)
            kv.append((k_ref[ks, :], v_ref[ks, :]))
        zs, sps, cums = [], [], []
        for kblk, _ in kv:
            for p in range(2):
                z = _dot_nt(kblk, qms[p]) + biases[p]
                sp = _softplus2(z)
                if masked:
                    sp = jnp.where(strict, sp, 0.0)
                zs.append(z)
                sps.append(sp)
        for sp in sps:
            cums.append(_dot(later, sp.astype(BF16)))
        accs = [carry[0], carry[2]]
        runs = [carry[1], carry[3]]
        for b, (_, vblk) in enumerate(kv):
            for p in range(2):
                i = 2 * b + p
                cum = cums[i] + runs[p]
                a = jnp.exp2(zs[i] - sps[i] - cum)
                if masked:
                    a = jnp.where(strict, a, 0.0)
                accs[p] = accs[p] + _dot_tn(vblk, a.astype(BF16))
                runs[p] = cum[0:1, :] + sps[i][0:1, :]
        return accs[0], runs[0], accs[1], runs[1]

    zacc = jnp.zeros((LANES, tq), F32)
    zrun = jnp.zeros((1, tq), F32)
    carry = key_blocks([qi], True, (zacc, zrun, zacc, zrun))
    odd = qi % 2
    carry = lax.fori_loop(0, odd, lambda _, c: key_blocks([qi - 1], False, c), carry)

    def two_blocks(jj, c):
        j_hi = qi - odd - 1 - 2 * jj
        return key_blocks([j_hi, j_hi - 1], False, c)

    carry = lax.fori_loop(0, qi // 2, two_blocks, carry)
    rows = lax.broadcasted_iota(I32, (LANES, tq), 0)
    o_t = jnp.where(rows < SB_HEAD_DIM, carry[0], carry[2])
    o_ref[...] = o_t.T.astype(BF16)


def _sb_prompt_call(qb, kb, vb, bias, batch, seq, tq):
    n = qb.shape[0]
    nq = seq // tq
    return pl.pallas_call(
        functools.partial(_sb_prompt_kernel, tq=tq),
        grid=(batch, SB_HEADS // 2, nq),
        in_specs=[pl.BlockSpec(memory_space=pltpu.SMEM),
                  pl.BlockSpec((tq, LANES), lambda b, p, i: (b * nq + i, p)),
                  pl.BlockSpec((seq, LANES), lambda b, p, i: (b, p)),
                  pl.BlockSpec((seq, LANES), lambda b, p, i: (b, p))],
        out_specs=pl.BlockSpec((tq, LANES), lambda b, p, i: (b * nq + i, p)),
        out_shape=jax.ShapeDtypeStruct((n, SB_WIDTH), BF16),
        compiler_params=_params(3),
        name="sb_prompt",
    )(bias, qb, kb, vb)


def _sb_decode_kernel(pt_ref, qc_ref, bias_ref, tri_ref, *refs, n_pages):
    k_refs = refs[:n_pages]
    v_refs = refs[n_pages:2 * n_pages]
    o_ref, acc_sc, run_sc, z_sc = refs[2 * n_pages:]
    g = pl.program_id(1)
    heads, d, page = k_refs[0].shape

    @pl.when(g == 0)
    def _():
        acc_sc[...] = jnp.zeros_like(acc_sc)
        run_sc[...] = jnp.zeros_like(run_sc)

    for i in range(n_pages):
        for h in range(heads):
            prod = k_refs[i][h] * qc_ref[0, h]
            r = i * heads + h
            z_sc[r:r + 1, :] = jnp.sum(prod, axis=0, keepdims=True)
    z = z_sc[...] + bias_ref[...]
    sp = _softplus2(z)
    cum = _dot(sp.astype(BF16), tri_ref[...])
    tot = jnp.sum(sp, axis=1, keepdims=True)
    lw = z - sp - cum
    run = run_sc[:, 0:1]
    weights = []
    for i in range(n_pages):
        rows = slice(i * heads, (i + 1) * heads)
        weights.append(jnp.exp2(lw[rows] - run))
        run = run + tot[rows]
    run_sc[...] = jnp.broadcast_to(run, run_sc.shape)
    for h in range(heads):
        acc = acc_sc[h]
        for i in range(n_pages):
            acc = acc + v_refs[i][h] * weights[i][h:h + 1, :]
        acc_sc[h] = acc

    @pl.when(g == pl.num_programs(1) - 1)
    def _():
        ones = jnp.ones((8, page), F32)
        for h in range(heads):
            tot = lax.dot_general(ones, acc_sc[h], (((1,), (1,)), ((), ())), preferred_element_type=F32,
                                  precision=lax.Precision.HIGHEST)
            o_ref[0, h:h + 1, :] = tot[0:1, :]


def _sb_decode_call(qc, bias, tri, cache_k_t, cache_v_t, page_table, layer):
    nb, heads, d, page = qc.shape
    n_seq_pages = page_table.shape[1]
    gp = DECODE_PAGES
    while n_seq_pages % gp:
        gp //= 2
    ng = n_seq_pages // gp
    bias_col = jnp.tile(bias.reshape(heads, 1), (gp, 1))

    def page_spec(i):
        def imap(b, g, pt):
            return (layer, pt[b, n_seq_pages - 1 - (g * gp + i)], 0, 0, 0)
        return pl.BlockSpec((None, None, heads, d, page), imap)

    grid_spec = pltpu.PrefetchScalarGridSpec(
        num_scalar_prefetch=1,
        grid=(nb, ng),
        in_specs=[pl.BlockSpec((1, heads, d, page), lambda b, g, pt: (b, 0, 0, 0)),
                  pl.BlockSpec((gp * heads, 1), lambda b, g, pt: (0, 0)),
                  pl.BlockSpec(tri.shape, lambda b, g, pt: (0, 0))]
                 + [page_spec(i) for i in range(gp)] * 2,
        out_specs=pl.BlockSpec((1, heads, d), lambda b, g, pt: (b, 0, 0)),
        scratch_shapes=[pltpu.VMEM((heads, d, page), F32), pltpu.VMEM((heads, LANES), F32),
                        pltpu.VMEM((gp * heads, page), F32)],
    )
    return pl.pallas_call(
        functools.partial(_sb_decode_kernel, n_pages=gp),
        grid_spec=grid_spec,
        out_shape=jax.ShapeDtypeStruct((nb, heads, d), F32),
        compiler_params=_params(2),
        name="sb_decode",
    )(page_table, qc, bias_col, tri, *([cache_k_t] * gp), *([cache_v_t] * gp))


def _merge_kernel(oa_ref, ob_ref, sga_ref, sgb_ref, x_ref, g1_ref, sh2_ref, sc2_ref,
                  wa_ref, wb_ref, wo_ref, gpm_ref, gpf_ref, wr_ref, br_ref,
                  x1_ref, h_ref, ti_ref, tp_ref):
    br_a = _dot(oa_ref[...], wa_ref[...])
    br_b = _dot(ob_ref[...], wb_ref[...])
    m = sga_ref[...].astype(F32) * br_a + sgb_ref[...].astype(F32) * br_b
    mixed = _dot(m.astype(BF16), wo_ref[...])
    x1 = x_ref[...] + g1_ref[0] * _rms(mixed, gpm_ref[...])
    x1_ref[...] = x1
    h = _rms(x1, gpf_ref[...]) * (1.0 + sc2_ref[0]) + sh2_ref[0]
    h_ref[...] = h

    logits = lax.dot_general(wr_ref[...], h, (((1,), (1,)), ((), ())), preferred_element_type=F32,
                             precision=lax.Precision.HIGHEST) + br_ref[...]
    eidx = lax.broadcasted_iota(I32, logits.shape, 0).astype(F32)
    vals = []
    for kk in range(TOP_K):
        mx = jnp.max(logits, axis=0, keepdims=True)
        sel = jnp.min(jnp.where(logits == mx, eidx, float(N_EXPERTS)), axis=0, keepdims=True)
        ti_ref[kk:kk + 1, :] = sel.astype(I32)
        vals.append(mx)
        logits = jnp.where(eidx == sel, -jnp.inf, logits)
    es = [jnp.exp(v - vals[0]) for v in vals]
    tot = es[0] + es[1] + es[2] + es[3]
    for kk in range(TOP_K):
        tp_ref[kk:kk + 1, :] = es[kk] / tot


def _merge_call(oa, ob, sga, sgb, x2d, gate1, shift2, scale2, w, tm, tiles_per_mod):
    n = x2d.shape[0]
    row = lambda i: (i, 0)
    colb = lambda i: (0, i)
    consts = [w["gla_o"], w["sb_o"], w["out"], w["g_post_mix"], w["g_pre_ffn"], w["router_t"], w["b_router"]]
    return pl.pallas_call(
        _merge_kernel,
        grid=(n // tm,),
        in_specs=[pl.BlockSpec((tm, GLA_VAL), row), pl.BlockSpec((tm, SB_WIDTH), row),
                  pl.BlockSpec((tm, D_MODEL), row), pl.BlockSpec((tm, D_MODEL), row),
                  pl.BlockSpec((tm, D_MODEL), row),
                  _mod_spec(gate1, tiles_per_mod), _mod_spec(shift2, tiles_per_mod),
                  _mod_spec(scale2, tiles_per_mod)] + [_const_spec(a) for a in consts],
        out_specs=[pl.BlockSpec((tm, D_MODEL), row), pl.BlockSpec((tm, D_MODEL), row),
                   pl.BlockSpec((TOP_K, tm), colb), pl.BlockSpec((TOP_K, tm), colb)],
        out_shape=[jax.ShapeDtypeStruct((n, D_MODEL), F32), jax.ShapeDtypeStruct((n, D_MODEL), F32),
                   jax.ShapeDtypeStruct((TOP_K, n), I32), jax.ShapeDtypeStruct((TOP_K, n), F32)],
        compiler_params=_params(1),
        name="merge",
    )(oa, ob, sga, sgb, x2d, gate1, shift2, scale2, *consts)


def _row_copy(src, src_row, dst, dst_row, sem):
    return pltpu.make_async_copy(src.at[pl.ds(src_row, 1), :], dst.at[pl.ds(dst_row, 1), :], sem)


def _dispatch_kernel(dest_ref, h_ref, xb_in_ref, xb_ref, idx_smem, idx_sem, row_sem):
    del xb_in_ref
    tm = h_ref.shape[0]
    idx_copy = pltpu.make_async_copy(dest_ref, idx_smem, idx_sem)
    idx_copy.start()
    idx_copy.wait()

    def body(t, carry):
        for kk in range(TOP_K):
            _row_copy(h_ref, t, xb_ref, idx_smem[kk, t], row_sem).start()
        return carry

    lax.fori_loop(0, tm, body, 0, unroll=ROW_COPY_UNROLL)
    for kk in range(TOP_K):
        pltpu.make_async_copy(h_ref, xb_ref.at[pl.ds(0, tm), :], row_sem).wait()


def _dispatch_call(dest, h, xb_zero, tm):
    n = h.shape[0]
    return pl.pallas_call(
        _dispatch_kernel,
        grid=(n // tm,),
        in_specs=[pl.BlockSpec((TOP_K, tm), lambda i: (0, i)),
                  pl.BlockSpec((tm, D_MODEL), lambda i: (i, 0)),
                  pl.BlockSpec(memory_space=pl.ANY)],
        out_specs=pl.BlockSpec(memory_space=pl.ANY),
        out_shape=jax.ShapeDtypeStruct(xb_zero.shape, F32),
        scratch_shapes=[pltpu.SMEM((TOP_K, tm), I32), pltpu.SemaphoreType.DMA, pltpu.SemaphoreType.DMA],
        input_output_aliases={2: 0},
        compiler_params=_params(1),
        name="dispatch",
    )(dest, h, xb_zero)


def _moe_kernel(be_ref, x_ref, wgu_ref, bgu_ref, wd_ref, bd_ref, y_ref, wgu_sc, wd_sc):
    b = pl.program_id(0)
    first_of_expert = (b == 0) | (be_ref[b] != be_ref[jnp.maximum(b - 1, 0)])

    @pl.when(first_of_expert)
    def _():
        wgu_sc[...] = wgu_ref[0].astype(BF16)
        wd_sc[...] = wd_ref[0].astype(BF16)

    x = x_ref[...].astype(BF16)
    gu = _dot(x, wgu_sc[...]) + bgu_ref[0]
    g = jnp.minimum(gu[:, :EXPERT_FF], SWIGLU_LIMIT)
    u = jnp.clip(gu[:, EXPERT_FF:], -SWIGLU_LIMIT, SWIGLU_LIMIT)
    act = (u + 1.0) * (g * _sigmoid(SWIGLU_ALPHA * g))
    y_ref[...] = _dot(act.astype(BF16), wd_sc[...]) + bd_ref[0]


def _moe_call(block_e, xb, w):
    r = xb.shape[0]
    bm = MOE_ROWS
    emap = lambda i, be: (be[i], 0, 0)
    grid_spec = pltpu.PrefetchScalarGridSpec(
        num_scalar_prefetch=1,
        grid=(r // bm,),
        in_specs=[pl.BlockSpec((bm, D_MODEL), lambda i, be: (i, 0)),
                  pl.BlockSpec((1, D_MODEL, 2 * EXPERT_FF), emap),
                  pl.BlockSpec((1, 1, 2 * EXPERT_FF), emap),
                  pl.BlockSpec((1, EXPERT_FF, D_MODEL), emap),
                  pl.BlockSpec((1, 1, D_MODEL), emap)],
        out_specs=pl.BlockSpec((bm, D_MODEL), lambda i, be: (i, 0)),
        scratch_shapes=[pltpu.VMEM((D_MODEL, 2 * EXPERT_FF), BF16), pltpu.VMEM((EXPERT_FF, D_MODEL), BF16)],
    )
    return pl.pallas_call(
        _moe_kernel,
        grid_spec=grid_spec,
        out_shape=jax.ShapeDtypeStruct((r, D_MODEL), F32),
        compiler_params=_params(1),
        name="moe",
    )(block_e, xb, w["gate_up"], w["b_gate_up"], w["down"], w["b_down"])


def _final_kernel(dest_ref, p_ref, x1_ref, g2_ref, gpf_ref, yb_ref, out_ref, idx_smem, ybuf, idx_sem, row_sem):
    tm = x1_ref.shape[0]
    idx_copy = pltpu.make_async_copy(dest_ref, idx_smem, idx_sem)
    idx_copy.start()
    idx_copy.wait()

    def body(t, carry):
        for kk in range(TOP_K):
            _row_copy(yb_ref, idx_smem[kk, t], ybuf.at[kk], t, row_sem).start()
        return carry

    lax.fori_loop(0, tm, body, 0, unroll=ROW_COPY_UNROLL)
    for kk in range(TOP_K):
        pltpu.make_async_copy(yb_ref.at[pl.ds(0, tm), :], ybuf.at[kk], row_sem).wait()

    p = p_ref[...]
    f = p[:, 0:1] * ybuf[0]
    for kk in range(1, TOP_K):
        f = f + p[:, kk:kk + 1] * ybuf[kk]
    out_ref[...] = x1_ref[...] + g2_ref[0] * _rms(f, gpf_ref[...])


def _final_call(dest, probs_t, x1, gate2, g_post_ffn, yb, tm, tiles_per_mod):
    n = x1.shape[0]
    return pl.pallas_call(
        _final_kernel,
        grid=(n // tm,),
        in_specs=[pl.BlockSpec((TOP_K, tm), lambda i: (0, i)),
                  pl.BlockSpec((tm, TOP_K), lambda i: (i, 0)),
                  pl.BlockSpec((tm, D_MODEL), lambda i: (i, 0)),
                  _mod_spec(gate2, tiles_per_mod), _const_spec(g_post_ffn),
                  pl.BlockSpec(memory_space=pl.ANY)],
        out_specs=pl.BlockSpec((tm, D_MODEL), lambda i: (i, 0)),
        out_shape=jax.ShapeDtypeStruct((n, D_MODEL), F32),
        scratch_shapes=[pltpu.SMEM((TOP_K, tm), I32), pltpu.VMEM((TOP_K, tm, D_MODEL), F32),
                        pltpu.SemaphoreType.DMA, pltpu.SemaphoreType.DMA],
        compiler_params=_params(1),
        name="final",
    )(dest, probs_t, x1, gate2, g_post_ffn, yb)


def _route(top_i, bm):
    k, n = top_i.shape
    nk = k * n
    ef = top_i.reshape(nk)
    oh = (ef[:, None] == jnp.arange(N_EXPERTS, dtype=I32)[None, :]).astype(I32)
    cs = jnp.cumsum(oh, axis=0)
    rank = jnp.sum((cs - oh) * oh, axis=1)
    counts = cs[-1]
    padded = (counts + bm - 1) // bm * bm
    pend = jnp.cumsum(padded)
    pstart = pend - padded
    dest = jnp.sum(oh * pstart[None, :], axis=1) + rank
    n_blocks = -(-nk // bm) + N_EXPERTS
    first_row = jnp.arange(n_blocks, dtype=I32) * bm
    block_e = jnp.sum((pend[None, :] <= first_row[:, None]).astype(I32), axis=1)
    block_e = jnp.minimum(block_e, N_EXPERTS - 1).astype(I32)
    return dest.reshape(k, n).astype(I32), block_e, n_blocks * bm


def _pick_tile(n, pref):
    t = pref
    while n % t:
        t //= 2
    return t


def _layer(x, ada, s0, w, sb_fn):
    batch, seq, _ = x.shape
    n = batch * seq
    x2d = x.reshape(n, D_MODEL)
    if seq == 1:
        tm = _pick_tile(n, 128)
        mods = [ada[:, i].reshape(n // tm, tm, D_MODEL) for i in range(N_ADA)]
        tiles_per_mod = 1
    else:
        tm = _pick_tile(seq, 256)
        mods = [ada[:, i].reshape(batch, 1, D_MODEL) for i in range(N_ADA)]
        tiles_per_mod = seq // tm
    shift1, scale1, gate1, shift2, scale2, gate2 = mods

    kv_batch = batch if seq % LANES == 0 else None
    qa, ka, la, va, ra, qb, kb, vb, kf, vf, sga, sgb = _inproj_call(
        x2d, shift1, scale1, w["g_pre_mix"], w, tm, tiles_per_mod, kv_batch)

    seq_p = -(-seq // GLA_CHUNK) * GLA_CHUNK
    if seq_p != seq:
        def pad_t(a):
            a = a.reshape(batch, seq, a.shape[-1])
            return jnp.pad(a, ((0, 0), (0, seq_p - seq), (0, 0))).reshape(batch * seq_p, a.shape[-1])
        gq, gk, gl, gv, gr = [pad_t(a) for a in (qa, ka, la, va, ra)]
    else:
        gq, gk, gl, gv, gr = qa, ka, la, va, ra
    tt = _pick_tile(seq_p, 512)
    oa, s_new = _gla_call(gq, gk, gl, gv, gr, s0, w["g_gla_head"], batch, seq_p, tt)
    if seq_p != seq:
        oa = oa.reshape(batch, seq_p, GLA_VAL)[:, :seq].reshape(n, GLA_VAL)

    ob = sb_fn(qb, kb, vb)

    x1, h, top_i, top_p = _merge_call(oa, ob, sga, sgb, x2d, gate1, shift2, scale2, w, tm, tiles_per_mod)

    dest, block_e, n_rows = _route(top_i, MOE_ROWS)
    xb = _dispatch_call(dest, h, jnp.zeros((n_rows, D_MODEL), F32), tm)
    yb = _moe_call(block_e, xb, w)
    y = _final_call(dest, top_p.T, x1, gate2, w["g_post_ffn"], yb, tm, tiles_per_mod)

    if kv_batch is not None:
        kf, vf = [jnp.transpose(a, (0, 3, 1, 2)) for a in (kf, vf)]
    kv_shape = (batch, seq, SB_HEADS, SB_HEAD_DIM)
    return y.reshape(batch, seq, D_MODEL), kf.reshape(kv_shape), vf.reshape(kv_shape), s_new


def _layer_weights(l, w_in, w_alpha, b_alpha, g_pre_mix, g_gla_head, w_gla_o, w_sb_o, w_out, g_post_mix,
                   g_pre_ffn, w_router, b_router, w_gate_up, b_gate_up, w_down, b_down, g_post_ffn):
    wi = w_in[l]
    o_lr = 2 * GLA_KEY + 2 * GLA_VAL
    o_sb = o_lr + GLA_LOWRANK
    o_gt = o_sb + 3 * SB_WIDTH
    w_lr = jnp.pad(wi[:, o_lr:o_sb], ((0, 0), (0, LANES - GLA_LOWRANK)))
    w_al = jnp.pad(w_alpha[l], ((0, LANES - GLA_LOWRANK), (0, 0)))
    row = lambda a: a[l].reshape(1, -1)
    return {
        "gla": wi[:, :o_lr].astype(BF16), "lr": w_lr.astype(BF16), "alpha": w_al.astype(BF16),
        "b_alpha": row(b_alpha), "sb": wi[:, o_sb:o_gt].astype(BF16), "gates": wi[:, o_gt:].astype(BF16),
        "g_pre_mix": row(g_pre_mix), "g_gla_head": row(g_gla_head),
        "gla_o": w_gla_o[l].astype(BF16), "sb_o": w_sb_o[l].astype(BF16), "out": w_out[l].astype(BF16),
        "g_post_mix": row(g_post_mix), "g_pre_ffn": row(g_pre_ffn),
        "router_t": w_router[l].T, "b_router": b_router[l].reshape(N_EXPERTS, 1),
        "gate_up": w_gate_up[l], "b_gate_up": b_gate_up[l].reshape(N_EXPERTS, 1, -1),
        "down": w_down[l], "b_down": b_down[l].reshape(N_EXPERTS, 1, -1),
        "g_post_ffn": row(g_post_ffn),
    }


def kernel(x_prompt, x_sample, cache_sb_k, cache_sb_v, state_gla, page_table, c_prompt, c_sample, w_ada, b_ada, g_pre_mix, w_in, w_alpha, b_alpha, g_gla_head, w_gla_o, b_sb_logit, w_sb_o, w_out, g_post_mix, g_pre_ffn, w_router, b_router, w_gate_up, b_gate_up, w_down, b_down, g_post_ffn):
    depth = w_in.shape[0]
    bp, sp, _ = x_prompt.shape
    bs, ss, _ = x_sample.shape
    assert ss == 1, "the sample group decodes one token per sequence"
    page = cache_sb_k.shape[2]
    ridx = jnp.arange(page, dtype=I32)
    tri = (ridx[:, None] > ridx[None, :]).astype(BF16)
    cache_k_t = jnp.transpose(cache_sb_k, (0, 1, 3, 4, 2))
    cache_v_t = jnp.transpose(cache_sb_v, (0, 1, 3, 4, 2))

    yp, ys = x_prompt, x_sample
    outs = [[] for _ in range(6)]
    for l in range(depth):
        w = _layer_weights(l, w_in, w_alpha, b_alpha, g_pre_mix, g_gla_head, w_gla_o, w_sb_o, w_out, g_post_mix,
                           g_pre_ffn, w_router, b_router, w_gate_up, b_gate_up, w_down, b_down, g_post_ffn)
        c_all = jnp.concatenate([c_prompt, c_sample], axis=0)
        ada = _ada_call(c_all, w_ada[l], b_ada[l].reshape(1, -1))
        ada_p = ada[:bp].reshape(bp, N_ADA, D_MODEL)
        ada_s = ada[bp:].reshape(bs, N_ADA, D_MODEL)
        bias = b_sb_logit[l] * LOG2E

        tq = _pick_tile(sp, 256)
        sb_prompt = lambda qb, kb, vb: _sb_prompt_call(qb, kb, vb, bias, bp, sp, tq)
        s0_p = jnp.zeros((bp, GLA_HEADS, GLA_HEAD_K, GLA_HEAD_V), F32)
        yp, kp, vp, stp = _layer(yp, ada_p, s0_p, w, sb_prompt)

        def sb_sample(qb, kb, vb):
            q3 = qb.reshape(bs, SB_HEADS, SB_HEAD_DIM).astype(F32)
            qc = jnp.broadcast_to(q3[..., None], q3.shape + (page,))
            o = _sb_decode_call(qc, bias, tri, cache_k_t, cache_v_t, page_table, l)
            return o.reshape(bs, SB_WIDTH).astype(BF16)

        ys, kn, vn, stn = _layer(ys, ada_s, state_gla[l], w, sb_sample)
        for lst, val in zip(outs, (kp, vp, stp, kn, vn, stn)):
            lst.append(val)
    return (yp, ys) + tuple(jnp.stack(o) for o in outs)
```

```python
import functools

import jax
import jax.numpy as jnp
from jax import lax
from jax.experimental import pallas as pl
from jax.experimental.pallas import tpu as pltpu

F32 = jnp.float32
BF16 = jnp.bfloat16
I32 = jnp.int32

D_MODEL = 1024
GLA_HEADS = 4
GLA_HEAD_K = 64
GLA_HEAD_V = 128
GLA_KEY = GLA_HEADS * GLA_HEAD_K
GLA_VAL = GLA_HEADS * GLA_HEAD_V
GLA_LOWRANK = 16
GLA_TAU = 16.0
GLA_CHUNK = 64
SB_HEADS = 8
SB_HEAD_DIM = 64
SB_WIDTH = SB_HEADS * SB_HEAD_DIM
N_EXPERTS = 32
TOP_K = 4
EXPERT_FF = D_MODEL
SWIGLU_LIMIT = 7.0
SWIGLU_ALPHA = 1.702
RMS_EPS = 1e-6
N_ADA = 6

LOG2E = 1.4426950408889634
LANES = 128
VMEM_LIMIT = 56 * 1024 * 1024
MOE_ROWS = 256
DECODE_PAGES = 8
ROW_COPY_UNROLL = 8


def _params(n_axes, vmem=VMEM_LIMIT):
    return pltpu.CompilerParams(dimension_semantics=("arbitrary",) * n_axes, vmem_limit_bytes=vmem)


def _sigmoid(x):
    return 1.0 / (1.0 + jnp.exp(-x))


def _softplus(z):
    return jnp.maximum(z, 0.0) + jnp.log(1.0 + jnp.exp(-jnp.abs(z)))


def _softplus2(z2):
    return jnp.maximum(z2, 0.0) + jnp.log2(1.0 + jnp.exp2(-jnp.abs(z2)))


def _rms(x, g):
    ms = jnp.mean(x * x, axis=-1, keepdims=True)
    return x * lax.rsqrt(ms + RMS_EPS) * g


def _dot(a, b):
    return jnp.dot(a, b, preferred_element_type=F32)


def _dot_nt(a, b):
    return lax.dot_general(a, b, (((1,), (1,)), ((), ())), preferred_element_type=F32)


def _dot_tn(a, b):
    return lax.dot_general(a, b, (((0,), (0,)), ((), ())), preferred_element_type=F32)


def _ada_kernel(c_ref, w_ref, b_ref, o_ref):
    c = c_ref[...]
    s = c * _sigmoid(c)
    o_ref[...] = jnp.dot(s, w_ref[...], preferred_element_type=F32,
                         precision=lax.Precision.HIGHEST) + b_ref[...]


def _ada_call(c_all, w_ada, b_ada):
    rows = c_all.shape[0]
    n_out = w_ada.shape[1]
    tn = D_MODEL
    return pl.pallas_call(
        _ada_kernel,
        grid=(n_out // tn,),
        in_specs=[pl.BlockSpec((rows, D_MODEL), lambda j: (0, 0)),
                  pl.BlockSpec((D_MODEL, tn), lambda j: (0, j)),
                  pl.BlockSpec((1, tn), lambda j: (0, j))],
        out_specs=pl.BlockSpec((rows, tn), lambda j: (0, j)),
        out_shape=jax.ShapeDtypeStruct((rows, n_out), F32),
        compiler_params=_params(1),
        name="ada",
    )(c_all, w_ada, b_ada)


def _inproj_kernel(x_ref, sh_ref, sc_ref, g_ref, wg_ref, wlr_ref, wal_ref, bal_ref, wsb_ref, wgt_ref,
                   qa_ref, ka_ref, la_ref, va_ref, ra_ref, qb_ref, kb_ref, vb_ref, kf_ref, vf_ref,
                   sga_ref, sgb_ref, *, kv_transposed):
    x = x_ref[...]
    u = _rms(x, g_ref[...]) * (1.0 + sc_ref[0]) + sh_ref[0]
    ub = u.astype(BF16)

    pg = _dot(ub, wg_ref[...])
    qa_ref[...] = pg[:, :GLA_KEY] * (GLA_HEAD_K ** -0.5)
    ka_ref[...] = pg[:, GLA_KEY:2 * GLA_KEY]
    va_ref[...] = pg[:, 2 * GLA_KEY:2 * GLA_KEY + GLA_VAL].astype(BF16)
    r = pg[:, 2 * GLA_KEY + GLA_VAL:]
    ra_ref[...] = (r * _sigmoid(r)).astype(BF16)

    lr = _dot(ub, wlr_ref[...])
    al = _dot(lr.astype(BF16), wal_ref[...]) + bal_ref[...]
    la_ref[...] = -_softplus(-al) * (1.0 / GLA_TAU)

    ps = _dot(ub, wsb_ref[...])
    qb_ref[...] = (ps[:, :SB_WIDTH] * (LOG2E * SB_HEAD_DIM ** -0.5)).astype(BF16)
    k = ps[:, SB_WIDTH:2 * SB_WIDTH]
    v = ps[:, 2 * SB_WIDTH:]
    if kv_transposed:
        kf_ref[0] = k.T.reshape(SB_HEADS, SB_HEAD_DIM, k.shape[0])
        vf_ref[0] = v.T.reshape(SB_HEADS, SB_HEAD_DIM, v.shape[0])
    else:
        kf_ref[...] = k
        vf_ref[...] = v
    kb_ref[...] = k.astype(BF16)
    vb_ref[...] = v.astype(BF16)

    pt = _dot(ub, wgt_ref[...])
    sga_ref[...] = _sigmoid(pt[:, :D_MODEL]).astype(BF16)
    sgb_ref[...] = _sigmoid(pt[:, D_MODEL:]).astype(BF16)


def _mod_spec(mod, tiles_per_mod):
    return pl.BlockSpec((1,) + mod.shape[1:], lambda i: (i // tiles_per_mod, 0, 0))


def _const_spec(a):
    nd = a.ndim
    return pl.BlockSpec(a.shape, lambda *_: (0,) * nd)


def _inproj_call(x2d, shift, scale, gain, w, tm, tiles_per_mod, kv_batch):
    n = x2d.shape[0]
    widths = [(GLA_KEY, F32), (GLA_KEY, F32), (GLA_KEY, F32), (GLA_VAL, BF16), (GLA_VAL, BF16),
              (SB_WIDTH, BF16), (SB_WIDTH, BF16), (SB_WIDTH, BF16), (SB_WIDTH, F32), (SB_WIDTH, F32),
              (D_MODEL, BF16), (D_MODEL, BF16)]
    row = lambda i: (i, 0)
    out_specs = [pl.BlockSpec((tm, wd), row) for wd, _ in widths]
    out_shape = [jax.ShapeDtypeStruct((n, wd), dt) for wd, dt in widths]
    if kv_batch is not None:
        tiles = n // kv_batch // tm
        for idx in (8, 9):
            out_specs[idx] = pl.BlockSpec((1, SB_HEADS, SB_HEAD_DIM, tm), lambda i: (i // tiles, 0, 0, i % tiles))
            out_shape[idx] = jax.ShapeDtypeStruct((kv_batch, SB_HEADS, SB_HEAD_DIM, n // kv_batch), F32)
    consts = [gain, w["gla"], w["lr"], w["alpha"], w["b_alpha"], w["sb"], w["gates"]]
    return pl.pallas_call(
        functools.partial(_inproj_kernel, kv_transposed=kv_batch is not None),
        grid=(n // tm,),
        in_specs=[pl.BlockSpec((tm, D_MODEL), row), _mod_spec(shift, tiles_per_mod),
                  _mod_spec(scale, tiles_per_mod)] + [_const_spec(a) for a in consts],
        out_specs=out_specs,
        out_shape=out_shape,
        compiler_params=_params(1),
        name="inproj",
    )(x2d, shift, scale, *consts)


def _gla_kernel(q_ref, k_ref, la_ref, v_ref, r_ref, s0_ref, g_ref, o_ref, sn_ref, st_sc, *, n_chunks):
    j = pl.program_id(1)
    C = GLA_CHUNK

    @pl.when(j == 0)
    def _():
        for p in range(2):
            s_pair = jnp.concatenate([s0_ref[0, 2 * p], s0_ref[0, 2 * p + 1]], axis=0)
            st_sc[p] = s_pair.T

    row = lax.broadcasted_iota(I32, (C, C), 0)
    col = lax.broadcasted_iota(I32, (C, C), 1)
    causal = col <= row
    l_incl = jnp.where(causal, 1.0, 0.0).astype(BF16)
    lane = lax.broadcasted_iota(I32, (C, LANES), 1)
    lane_sq = lax.broadcasted_iota(I32, (LANES, LANES), 1)

    def chunk(c, carry):
        sl = pl.ds(pl.multiple_of(c * C, C), C)
        a = la_ref[sl, :]
        a_hi = a.astype(BF16)
        a_lo = (a - a_hi.astype(F32)).astype(BF16)
        b = _dot(l_incl, a_hi) + _dot(l_incl, a_lo)
        b_last = b[C - 1:C, :]
        q_t = q_ref[sl, :] * jnp.exp(b)
        k = k_ref[sl, :]
        k_t = (k * jnp.exp(-b)).astype(BF16)
        k_u = (k * jnp.exp(b_last - b)).astype(BF16)
        dec = jnp.exp(b_last)
        for p in range(2):
            ps = slice(LANES * p, LANES * (p + 1))
            qp = q_t[:, ps]
            kp = k_t[:, ps]
            kup = k_u[:, ps]
            st = st_sc[p]
            stb = st.astype(BF16)
            upd = []
            for hh in range(2):
                h = 2 * p + hh
                hs = slice(GLA_HEAD_V * h, GLA_HEAD_V * (h + 1))
                in_head = (lane >= GLA_HEAD_K * hh) & (lane < GLA_HEAD_K * (hh + 1))
                qm = jnp.where(in_head, qp, 0.0).astype(BF16)
                s = jnp.where(causal, _dot_nt(qm, kp), 0.0)
                vh = v_ref[sl, hs]
                o = _dot_nt(qm, stb) + _dot(s.astype(BF16), vh)
                on = _rms(o, g_ref[...])
                o_ref[sl, hs] = (on * r_ref[sl, hs].astype(F32)).astype(BF16)
                upd.append(_dot_tn(vh, kup))
            st_sc[p] = st * dec[:, ps] + jnp.where(lane_sq < GLA_HEAD_K, upd[0], upd[1])
        return carry

    lax.fori_loop(0, n_chunks, chunk, 0)

    @pl.when(j == pl.num_programs(1) - 1)
    def _():
        for p in range(2):
            s_pair = st_sc[p].T
            sn_ref[0, 2 * p] = s_pair[:GLA_HEAD_K]
            sn_ref[0, 2 * p + 1] = s_pair[GLA_HEAD_K:]


def _gla_call(qa, ka, la, va, ra, s0, g_head, batch, seq, tt):
    n = qa.shape[0]
    nt = seq // tt
    rowmap = lambda b, j: (b * nt + j, 0)
    smap = lambda b, j: (b, 0, 0, 0)
    sblock = (1, GLA_HEADS, GLA_HEAD_K, GLA_HEAD_V)
    return pl.pallas_call(
        functools.partial(_gla_kernel, n_chunks=tt // GLA_CHUNK),
        grid=(batch, nt),
        in_specs=[pl.BlockSpec((tt, GLA_KEY), rowmap), pl.BlockSpec((tt, GLA_KEY), rowmap),
                  pl.BlockSpec((tt, GLA_KEY), rowmap), pl.BlockSpec((tt, GLA_VAL), rowmap),
                  pl.BlockSpec((tt, GLA_VAL), rowmap), pl.BlockSpec(sblock, smap),
                  pl.BlockSpec((1, GLA_HEAD_V), lambda b, j: (0, 0))],
        out_specs=[pl.BlockSpec((tt, GLA_VAL), rowmap), pl.BlockSpec(sblock, smap)],
        out_shape=[jax.ShapeDtypeStruct((n, GLA_VAL), BF16),
                   jax.ShapeDtypeStruct((batch,) + sblock[1:], F32)],
        scratch_shapes=[pltpu.VMEM((2, LANES, LANES), F32)],
        compiler_params=_params(2),
        name="gla",
    )(qa, ka, la, va, ra, s0, g_head)


def _sb_prompt_kernel(bias_ref, q_ref, k_ref, v_ref, o_ref, *, tq):
    pair = pl.program_id(1)
    qi = pl.program_id(2)
    q2 = q_ref[...]
    lane = lax.broadcasted_iota(I32, (tq, LANES), 1)
    row = lax.broadcasted_iota(I32, (tq, tq), 0)
    col = lax.broadcasted_iota(I32, (tq, tq), 1)
    strict = row < col
    later = jnp.where(col > row, 1.0, 0.0).astype(BF16)
    qms = []
    for p in range(2):
        in_head = (lane >= SB_HEAD_DIM * p) & (lane < SB_HEAD_DIM * (p + 1))
        qms.append(jnp.where(in_head, q2, jnp.zeros_like(q2)))
    biases = [bias_ref[2 * pair], bias_ref[2 * pair + 1]]

    def key_blocks(js, masked, carry):
        kv = []
        for j in js:
            ks = pl.ds(pl.multiple_of(j * tq, tq), tq)
            kv.append((k_ref[ks, :], v_ref[ks, :]))
        zs, sps, cums = [], [], []
        for kblk, _ in kv:
            for p in range(2):
                z = _dot_nt(kblk, qms[p]) + biases[p]
                sp = _softplus2(z)
                if masked:
                    sp = jnp.where(strict, sp, 0.0)
                zs.append(z)
                sps.append(sp)
        for sp in sps:
            cums.append(_dot(later, sp.astype(BF16)))
        accs = [carry[0], carry[2]]
        runs = [carry[1], carry[3]]
        for b, (_, vblk) in enumerate(kv):
            for p in range(2):
                i = 2 * b + p
                cum = cums[i] + runs[p]
                a = jnp.exp2(zs[i] - sps[i] - cum)
                if masked:
                    a = jnp.where(strict, a, 0.0)
                accs[p] = accs[p] + _dot_tn(vblk, a.astype(BF16))
                runs[p] = cum[0:1, :] + sps[i][0:1, :]
        return accs[0], runs[0], accs[1], runs[1]

    zacc = jnp.zeros((LANES, tq), F32)
    zrun = jnp.zeros((1, tq), F32)
    carry = key_blocks([qi], True, (zacc, zrun, zacc, zrun))
    odd = qi % 2
    carry = lax.fori_loop(0, odd, lambda _, c: key_blocks([qi - 1], False, c), carry)

    def two_blocks(jj, c):
        j_hi = qi - odd - 1 - 2 * jj
        return key_blocks([j_hi, j_hi - 1], False, c)

    carry = lax.fori_loop(0, qi // 2, two_blocks, carry)
    rows = lax.broadcasted_iota(I32, (LANES, tq), 0)
    o_t = jnp.where(rows < SB_HEAD_DIM, carry[0], carry[2])
    o_ref[...] = o_t.T.astype(BF16)


def _sb_prompt_call(qb, kb, vb, bias, batch, seq, tq):
    n = qb.shape[0]
    nq = seq // tq
    return pl.pallas_call(
        functools.partial(_sb_prompt_kernel, tq=tq),
        grid=(batch, SB_HEADS // 2, nq),
        in_specs=[pl.BlockSpec(memory_space=pltpu.SMEM),
                  pl.BlockSpec((tq, LANES), lambda b, p, i: (b * nq + i, p)),
                  pl.BlockSpec((seq, LANES), lambda b, p, i: (b, p)),
                  pl.BlockSpec((seq, LANES), lambda b, p, i: (b, p))],
        out_specs=pl.BlockSpec((tq, LANES), lambda b, p, i: (b * nq + i, p)),
        out_shape=jax.ShapeDtypeStruct((n, SB_WIDTH), BF16),
        compiler_params=_params(3),
        name="sb_prompt",
    )(bias, qb, kb, vb)


def _sb_decode_kernel(pt_ref, qc_ref, bias_ref, tri_ref, *refs, n_pages):
    k_refs = refs[:n_pages]
    v_refs = refs[n_pages:2 * n_pages]
    o_ref, acc_sc, run_sc, z_sc = refs[2 * n_pages:]
    g = pl.program_id(1)
    heads, d, page = k_refs[0].shape

    @pl.when(g == 0)
    def _():
        acc_sc[...] = jnp.zeros_like(acc_sc)
        run_sc[...] = jnp.zeros_like(run_sc)

    for i in range(n_pages):
        for h in range(heads):
            prod = k_refs[i][h] * qc_ref[0, h]
            r = i * heads + h
            z_sc[r:r + 1, :] = jnp.sum(prod, axis=0, keepdims=True)
    z = z_sc[...] + bias_ref[...]
    sp = _softplus2(z)
    cum = _dot(sp.astype(BF16), tri_ref[...])
    tot = jnp.sum(sp, axis=1, keepdims=True)
    lw = z - sp - cum
    run = run_sc[:, 0:1]
    weights = []
    for i in range(n_pages):
        rows = slice(i * heads, (i + 1) * heads)
        weights.append(jnp.exp2(lw[rows] - run))
        run = run + tot[rows]
    run_sc[...] = jnp.broadcast_to(run, run_sc.shape)
    for h in range(heads):
        acc = acc_sc[h]
        for i in range(n_pages):
            acc = acc + v_refs[i][h] * weights[i][h:h + 1, :]
        acc_sc[h] = acc

    @pl.when(g == pl.num_programs(1) - 1)
    def _():
        ones = jnp.ones((8, page), F32)
        for h in range(heads):
            tot = lax.dot_general(ones, acc_sc[h], (((1,), (1,)), ((), ())), preferred_element_type=F32,
                                  precision=lax.Precision.HIGHEST)
            o_ref[0, h:h + 1, :] = tot[0:1, :]


def _sb_decode_call(qc, bias, tri, cache_k_t, cache_v_t, page_table, layer):
    nb, heads, d, page = qc.shape
    n_seq_pages = page_table.shape[1]
    gp = DECODE_PAGES
    while n_seq_pages % gp:
        gp //= 2
    ng = n_seq_pages // gp
    bias_col = jnp.tile(bias.reshape(heads, 1), (gp, 1))

    def page_spec(i):
        def imap(b, g, pt):
            return (layer, pt[b, n_seq_pages - 1 - (g * gp + i)], 0, 0, 0)
        return pl.BlockSpec((None, None, heads, d, page), imap)

    grid_spec = pltpu.PrefetchScalarGridSpec(
        num_scalar_prefetch=1,
        grid=(nb, ng),
        in_specs=[pl.BlockSpec((1, heads, d, page), lambda b, g, pt: (b, 0, 0, 0)),
                  pl.BlockSpec((gp * heads, 1), lambda b, g, pt: (0, 0)),
                  pl.BlockSpec(tri.shape, lambda b, g, pt: (0, 0))]
                 + [page_spec(i) for i in range(gp)] * 2,
        out_specs=pl.BlockSpec((1, heads, d), lambda b, g, pt: (b, 0, 0)),
        scratch_shapes=[pltpu.VMEM((heads, d, page), F32), pltpu.VMEM((heads, LANES), F32),
                        pltpu.VMEM((gp * heads, page), F32)],
    )
    return pl.pallas_call(
        functools.partial(_sb_decode_kernel, n_pages=gp),
        grid_spec=grid_spec,
        out_shape=jax.ShapeDtypeStruct((nb, heads, d), F32),
        compiler_params=_params(2),
        name="sb_decode",
    )(page_table, qc, bias_col, tri, *([cache_k_t] * gp), *([cache_v_t] * gp))


def _merge_kernel(oa_ref, ob_ref, sga_ref, sgb_ref, x_ref, g1_ref, sh2_ref, sc2_ref,
                  wa_ref, wb_ref, wo_ref, gpm_ref, gpf_ref, wr_ref, br_ref,
                  x1_ref, h_ref, ti_ref, tp_ref):
    br_a = _dot(oa_ref[...], wa_ref[...])
    br_b = _dot(ob_ref[...], wb_ref[...])
    m = sga_ref[...].astype(F32) * br_a + sgb_ref[...].astype(F32) * br_b
    mixed = _dot(m.astype(BF16), wo_ref[...])
    x1 = x_ref[...] + g1_ref[0] * _rms(mixed, gpm_ref[...])
    x1_ref[...] = x1
    h = _rms(x1, gpf_ref[...]) * (1.0 + sc2_ref[0]) + sh2_ref[0]
    h_ref[...] = h

    logits = lax.dot_general(wr_ref[...], h, (((1,), (1,)), ((), ())), preferred_element_type=F32,
                             precision=lax.Precision.HIGHEST) + br_ref[...]
    eidx = lax.broadcasted_iota(I32, logits.shape, 0).astype(F32)
    vals = []
    for kk in range(TOP_K):
        mx = jnp.max(logits, axis=0, keepdims=True)
        sel = jnp.min(jnp.where(logits == mx, eidx, float(N_EXPERTS)), axis=0, keepdims=True)
        ti_ref[kk:kk + 1, :] = sel.astype(I32)
        vals.append(mx)
        logits = jnp.where(eidx == sel, -jnp.inf, logits)
    es = [jnp.exp(v - vals[0]) for v in vals]
    tot = es[0] + es[1] + es[2] + es[3]
    for kk in range(TOP_K):
        tp_ref[kk:kk + 1, :] = es[kk] / tot


def _merge_call(oa, ob, sga, sgb, x2d, gate1, shift2, scale2, w, tm, tiles_per_mod):
    n = x2d.shape[0]
    row = lambda i: (i, 0)
    colb = lambda i: (0, i)
    consts = [w["gla_o"], w["sb_o"], w["out"], w["g_post_mix"], w["g_pre_ffn"], w["router_t"], w["b_router"]]
    return pl.pallas_call(
        _merge_kernel,
        grid=(n // tm,),
        in_specs=[pl.BlockSpec((tm, GLA_VAL), row), pl.BlockSpec((tm, SB_WIDTH), row),
                  pl.BlockSpec((tm, D_MODEL), row), pl.BlockSpec((tm, D_MODEL), row),
                  pl.BlockSpec((tm, D_MODEL), row),
                  _mod_spec(gate1, tiles_per_mod), _mod_spec(shift2, tiles_per_mod),
                  _mod_spec(scale2, tiles_per_mod)] + [_const_spec(a) for a in consts],
        out_specs=[pl.BlockSpec((tm, D_MODEL), row), pl.BlockSpec((tm, D_MODEL), row),
                   pl.BlockSpec((TOP_K, tm), colb), pl.BlockSpec((TOP_K, tm), colb)],
        out_shape=[jax.ShapeDtypeStruct((n, D_MODEL), F32), jax.ShapeDtypeStruct((n, D_MODEL), F32),
                   jax.ShapeDtypeStruct((TOP_K, n), I32), jax.ShapeDtypeStruct((TOP_K, n), F32)],
        compiler_params=_params(1),
        name="merge",
    )(oa, ob, sga, sgb, x2d, gate1, shift2, scale2, *consts)


def _row_copy(src, src_row, dst, dst_row, sem):
    return pltpu.make_async_copy(src.at[pl.ds(src_row, 1), :], dst.at[pl.ds(dst_row, 1), :], sem)


def _dispatch_kernel(dest_ref, h_ref, xb_in_ref, xb_ref, idx_smem, idx_sem, row_sem):
    del xb_in_ref
    tm = h_ref.shape[0]
    idx_copy = pltpu.make_async_copy(dest_ref, idx_smem, idx_sem)
    idx_copy.start()
    idx_copy.wait()

    def body(t, carry):
        for kk in range(TOP_K):
            _row_copy(h_ref, t, xb_ref, idx_smem[kk, t], row_sem).start()
        return carry

    lax.fori_loop(0, tm, body, 0, unroll=ROW_COPY_UNROLL)
    for kk in range(TOP_K):
        pltpu.make_async_copy(h_ref, xb_ref.at[pl.ds(0, tm), :], row_sem).wait()


def _dispatch_call(dest, h, xb_zero, tm):
    n = h.shape[0]
    return pl.pallas_call(
        _dispatch_kernel,
        grid=(n // tm,),
        in_specs=[pl.BlockSpec((TOP_K, tm), lambda i: (0, i)),
                  pl.BlockSpec((tm, D_MODEL), lambda i: (i, 0)),
                  pl.BlockSpec(memory_space=pl.ANY)],
        out_specs=pl.BlockSpec(memory_space=pl.ANY),
        out_shape=jax.ShapeDtypeStruct(xb_zero.shape, F32),
        scratch_shapes=[pltpu.SMEM((TOP_K, tm), I32), pltpu.SemaphoreType.DMA, pltpu.SemaphoreType.DMA],
        input_output_aliases={2: 0},
        compiler_params=_params(1),
        name="dispatch",
    )(dest, h, xb_zero)


def _moe_kernel(be_ref, x_ref, wgu_ref, bgu_ref, wd_ref, bd_ref, y_ref, wgu_sc, wd_sc):
    b = pl.program_id(0)
    first_of_expert = (b == 0) | (be_ref[b] != be_ref[jnp.maximum(b - 1, 0)])

    @pl.when(first_of_expert)
    def _():
        wgu_sc[...] = wgu_ref[0].astype(BF16)
        wd_sc[...] = wd_ref[0].astype(BF16)

    x = x_ref[...].astype(BF16)
    gu = _dot(x, wgu_sc[...]) + bgu_ref[0]
    g = jnp.minimum(gu[:, :EXPERT_FF], SWIGLU_LIMIT)
    u = jnp.clip(gu[:, EXPERT_FF:], -SWIGLU_LIMIT, SWIGLU_LIMIT)
    act = (u + 1.0) * (g * _sigmoid(SWIGLU_ALPHA * g))
    y_ref[...] = _dot(act.astype(BF16), wd_sc[...]) + bd_ref[0]


def _moe_call(block_e, xb, w):
    r = xb.shape[0]
    bm = MOE_ROWS
    emap = lambda i, be: (be[i], 0, 0)
    grid_spec = pltpu.PrefetchScalarGridSpec(
        num_scalar_prefetch=1,
        grid=(r // bm,),
        in_specs=[pl.BlockSpec((bm, D_MODEL), lambda i, be: (i, 0)),
                  pl.BlockSpec((1, D_MODEL, 2 * EXPERT_FF), emap),
                  pl.BlockSpec((1, 1, 2 * EXPERT_FF), emap),
                  pl.BlockSpec((1, EXPERT_FF, D_MODEL), emap),
                  pl.BlockSpec((1, 1, D_MODEL), emap)],
        out_specs=pl.BlockSpec((bm, D_MODEL), lambda i, be: (i, 0)),
        scratch_shapes=[pltpu.VMEM((D_MODEL, 2 * EXPERT_FF), BF16), pltpu.VMEM((EXPERT_FF, D_MODEL), BF16)],
    )
    return pl.pallas_call(
        _moe_kernel,
        grid_spec=grid_spec,
        out_shape=jax.ShapeDtypeStruct((r, D_MODEL), F32),
        compiler_params=_params(1),
        name="moe",
    )(block_e, xb, w["gate_up"], w["b_gate_up"], w["down"], w["b_down"])


def _final_kernel(dest_ref, p_ref, x1_ref, g2_ref, gpf_ref, yb_ref, out_ref, idx_smem, ybuf, idx_sem, row_sem):
    tm = x1_ref.shape[0]
    idx_copy = pltpu.make_async_copy(dest_ref, idx_smem, idx_sem)
    idx_copy.start()
    idx_copy.wait()

    def body(t, carry):
        for kk in range(TOP_K):
            _row_copy(yb_ref, idx_smem[kk, t], ybuf.at[kk], t, row_sem).start()
        return carry

    lax.fori_loop(0, tm, body, 0, unroll=ROW_COPY_UNROLL)
    for kk in range(TOP_K):
        pltpu.make_async_copy(yb_ref.at[pl.ds(0, tm), :], ybuf.at[kk], row_sem).wait()

    p = p_ref[...]
    f = p[:, 0:1] * ybuf[0]
    for kk in range(1, TOP_K):
        f = f + p[:, kk:kk + 1] * ybuf[kk]
    out_ref[...] = x1_ref[...] + g2_ref[0] * _rms(f, gpf_ref[...])


def _final_call(dest, probs_t, x1, gate2, g_post_ffn, yb, tm, tiles_per_mod):
    n = x1.shape[0]
    return pl.pallas_call(
        _final_kernel,
        grid=(n // tm,),
        in_specs=[pl.BlockSpec((TOP_K, tm), lambda i: (0, i)),
                  pl.BlockSpec((tm, TOP_K), lambda i: (i, 0)),
                  pl.BlockSpec((tm, D_MODEL), lambda i: (i, 0)),
                  _mod_spec(gate2, tiles_per_mod), _const_spec(g_post_ffn),
                  pl.BlockSpec(memory_space=pl.ANY)],
        out_specs=pl.BlockSpec((tm, D_MODEL), lambda i: (i, 0)),
        out_shape=jax.ShapeDtypeStruct((n, D_MODEL), F32),
        scratch_shapes=[pltpu.SMEM((TOP_K, tm), I32), pltpu.VMEM((TOP_K, tm, D_MODEL), F32),
                        pltpu.SemaphoreType.DMA, pltpu.SemaphoreType.DMA],
        compiler_params=_params(1),
        name="final",
    )(dest, probs_t, x1, gate2, g_post_ffn, yb)


def _route(top_i, bm):
    k, n = top_i.shape
    nk = k * n
    ef = top_i.reshape(nk)
    oh = (ef[:, None] == jnp.arange(N_EXPERTS, dtype=I32)[None, :]).astype(I32)
    cs = jnp.cumsum(oh, axis=0)
    rank = jnp.sum((cs - oh) * oh, axis=1)
    counts = cs[-1]
    padded = (counts + bm - 1) // bm * bm
    pend = jnp.cumsum(padded)
    pstart = pend - padded
    dest = jnp.sum(oh * pstart[None, :], axis=1) + rank
    n_blocks = -(-nk // bm) + N_EXPERTS
    first_row = jnp.arange(n_blocks, dtype=I32) * bm
    block_e = jnp.sum((pend[None, :] <= first_row[:, None]).astype(I32), axis=1)
    block_e = jnp.minimum(block_e, N_EXPERTS - 1).astype(I32)
    return dest.reshape(k, n).astype(I32), block_e, n_blocks * bm


def _pick_tile(n, pref):
    t = pref
    while n % t:
        t //= 2
    return t


def _layer(x, ada, s0, w, sb_fn):
    batch, seq, _ = x.shape
    n = batch * seq
    x2d = x.reshape(n, D_MODEL)
    if seq == 1:
        tm = _pick_tile(n, 128)
        mods = [ada[:, i].reshape(n // tm, tm, D_MODEL) for i in range(N_ADA)]
        tiles_per_mod = 1
    else:
        tm = _pick_tile(seq, 256)
        mods = [ada[:, i].reshape(batch, 1, D_MODEL) for i in range(N_ADA)]
        tiles_per_mod = seq // tm
    shift1, scale1, gate1, shift2, scale2, gate2 = mods

    kv_batch = batch if seq % LANES == 0 else None
    qa, ka, la, va, ra, qb, kb, vb, kf, vf, sga, sgb = _inproj_call(
        x2d, shift1, scale1, w["g_pre_mix"], w, tm, tiles_per_mod, kv_batch)

    seq_p = -(-seq // GLA_CHUNK) * GLA_CHUNK
    if seq_p != seq:
        def pad_t(a):
            a = a.reshape(batch, seq, a.shape[-1])
            return jnp.pad(a, ((0, 0), (0, seq_p - seq), (0, 0))).reshape(batch * seq_p, a.shape[-1])
        gq, gk, gl, gv, gr = [pad_t(a) for a in (qa, ka, la, va, ra)]
    else:
        gq, gk, gl, gv, gr = qa, ka, la, va, ra
    tt = _pick_tile(seq_p, 512)
    oa, s_new = _gla_call(gq, gk, gl, gv, gr, s0, w["g_gla_head"], batch, seq_p, tt)
    if seq_p != seq:
        oa = oa.reshape(batch, seq_p, GLA_VAL)[:, :seq].reshape(n, GLA_VAL)

    ob = sb_fn(qb, kb, vb)

    x1, h, top_i, top_p = _merge_call(oa, ob, sga, sgb, x2d, gate1, shift2, scale2, w, tm, tiles_per_mod)

    dest, block_e, n_rows = _route(top_i, MOE_ROWS)
    xb = _dispatch_call(dest, h, jnp.zeros((n_rows, D_MODEL), F32), tm)
    yb = _moe_call(block_e, xb, w)
    y = _final_call(dest, top_p.T, x1, gate2, w["g_post_ffn"], yb, tm, tiles_per_mod)

    if kv_batch is not None:
        kf, vf = [jnp.transpose(a, (0, 3, 1, 2)) for a in (kf, vf)]
    kv_shape = (batch, seq, SB_HEADS, SB_HEAD_DIM)
    return y.reshape(batch, seq, D_MODEL), kf.reshape(kv_shape), vf.reshape(kv_shape), s_new


def _layer_weights(l, w_in, w_alpha, b_alpha, g_pre_mix, g_gla_head, w_gla_o, w_sb_o, w_out, g_post_mix,
                   g_pre_ffn, w_router, b_router, w_gate_up, b_gate_up, w_down, b_down, g_post_ffn):
    wi = w_in[l]
    o_lr = 2 * GLA_KEY + 2 * GLA_VAL
    o_sb = o_lr + GLA_LOWRANK
    o_gt = o_sb + 3 * SB_WIDTH
    w_lr = jnp.pad(wi[:, o_lr:o_sb], ((0, 0), (0, LANES - GLA_LOWRANK)))
    w_al = jnp.pad(w_alpha[l], ((0, LANES - GLA_LOWRANK), (0, 0)))
    row = lambda a: a[l].reshape(1, -1)
    return {
        "gla": wi[:, :o_lr].astype(BF16), "lr": w_lr.astype(BF16), "alpha": w_al.astype(BF16),
        "b_alpha": row(b_alpha), "sb": wi[:, o_sb:o_gt].astype(BF16), "gates": wi[:, o_gt:].astype(BF16),
        "g_pre_mix": row(g_pre_mix), "g_gla_head": row(g_gla_head),
        "gla_o": w_gla_o[l].astype(BF16), "sb_o": w_sb_o[l].astype(BF16), "out": w_out[l].astype(BF16),
        "g_post_mix": row(g_post_mix), "g_pre_ffn": row(g_pre_ffn),
        "router_t": w_router[l].T, "b_router": b_router[l].reshape(N_EXPERTS, 1),
        "gate_up": w_gate_up[l], "b_gate_up": b_gate_up[l].reshape(N_EXPERTS, 1, -1),
        "down": w_down[l], "b_down": b_down[l].reshape(N_EXPERTS, 1, -1),
        "g_post_ffn": row(g_post_ffn),
    }


def kernel(x_prompt, x_sample, cache_sb_k, cache_sb_v, state_gla, page_table, c_prompt, c_sample, w_ada, b_ada, g_pre_mix, w_in, w_alpha, b_alpha, g_gla_head, w_gla_o, b_sb_logit, w_sb_o, w_out, g_post_mix, g_pre_ffn, w_router, b_router, w_gate_up, b_gate_up, w_down, b_down, g_post_ffn):
    depth = w_in.shape[0]
    bp, sp, _ = x_prompt.shape
    bs, ss, _ = x_sample.shape
    assert ss == 1, "the sample group decodes one token per sequence"
    page = cache_sb_k.shape[2]
    ridx = jnp.arange(page, dtype=I32)
    tri = (ridx[:, None] > ridx[None, :]).astype(BF16)
    cache_k_t = jnp.transpose(cache_sb_k, (0, 1, 3, 4, 2))
    cache_v_t = jnp.transpose(cache_sb_v, (0, 1, 3, 4, 2))

    yp, ys = x_prompt, x_sample
    outs = [[] for _ in range(6)]
    for l in range(depth):
        w = _layer_weights(l, w_in, w_alpha, b_alpha, g_pre_mix, g_gla_head, w_gla_o, w_sb_o, w_out, g_post_mix,
                           g_pre_ffn, w_router, b_router, w_gate_up, b_gate_up, w_down, b_down, g_post_ffn)
        c_all = jnp.concatenate([c_prompt, c_sample], axis=0)
        ada = _ada_call(c_all, w_ada[l], b_ada[l].reshape(1, -1))
        ada_p = ada[:bp].reshape(bp, N_ADA, D_MODEL)
        ada_s = ada[bp:].reshape(bs, N_ADA, D_MODEL)
        bias = b_sb_logit[l] * LOG2E

        tq = _pick_tile(sp, 512)
        sb_prompt = lambda qb, kb, vb: _sb_prompt_call(qb, kb, vb, bias, bp, sp, tq)
        s0_p = jnp.zeros((bp, GLA_HEADS, GLA_HEAD_K, GLA_HEAD_V), F32)
        yp, kp, vp, stp = _layer(yp, ada_p, s0_p, w, sb_prompt)

        def sb_sample(qb, kb, vb):
            q3 = qb.reshape(bs, SB_HEADS, SB_HEAD_DIM).astype(F32)
            qc = jnp.broadcast_to(q3[..., None], q3.shape + (page,))
            o = _sb_decode_call(qc, bias, tri, cache_k_t, cache_v_t, page_table, l)
            return o.reshape(bs, SB_WIDTH).astype(BF16)

        ys, kn, vn, stn = _layer(ys, ada_s, state_gla[l], w, sb_sample)
        for lst, val in zip(outs, (kp, vp, stp, kn, vn, stn)):
            lst.append(val)
    return (yp, ys) + tuple(jnp.stack(o) for o in outs)
```
